```python
import math
import jax, jax.numpy as jnp
from jax import lax
import numpy as np


D_MODEL = 2048
BATCH = 8
SEQ = 2048
DEPTH = 1

N_MEM = 256
EPS = 1e-5
CONV_CH = D_MODEL // 2
CONV_KERNEL = 31
SGU_WIDTH = D_MODEL // 2
SGU_GROUPS = 8
SGU_GROUP_DIM = SGU_WIDTH // SGU_GROUPS
CHUNK = 128
XATTN_HEADS = 4
XATTN_HEAD_DIM = D_MODEL // 8
XATTN_WIDTH = XATTN_HEADS * XATTN_HEAD_DIM
N_BRANCH = 3
IN_COLS = 2 * CONV_CH + 2 * SGU_WIDTH + XATTN_WIDTH + N_BRANCH * D_MODEL
N_EXPERTS = 32
TOP_K = 4
D_EXPERT = D_MODEL
SWIGLU_LIMIT = 7.0
SWIGLU_ALPHA = 1.702
EXPERT_BLOCK = 256

kernel_name = 'hybrid_conv_sgu_memxattn_moe_block'


def rms_norm(x, g):
    x32 = x.astype(jnp.float32)
    y = x32 * lax.rsqrt(jnp.mean(x32 * x32, axis=-1, keepdims=True) + EPS)
    return (y * g.astype(jnp.float32)).astype(x.dtype)


def layer_norm(x, g, b):
    x32 = x.astype(jnp.float32)
    mu = jnp.mean(x32, axis=-1, keepdims=True)
    var = jnp.mean(jnp.square(x32 - mu), axis=-1, keepdims=True)
    y = (x32 - mu) * lax.rsqrt(var + EPS)
    return (y * g.astype(jnp.float32) + b.astype(jnp.float32)).astype(x.dtype)


def conformer_conv(a, w_dw, b_dw, ln_g, ln_b, w_o):
    p, q = jnp.split(a, 2, axis=-1)
    h = p * jax.nn.sigmoid(q)
    h = lax.conv_general_dilated(
        h, w_dw[:, None, :], window_strides=(1,),
        padding=[(CONV_KERNEL - 1, 0)],
        dimension_numbers=('NWC', 'WIO', 'NWC'),
        feature_group_count=CONV_CH) + b_dw
    h = jax.nn.silu(layer_norm(h, ln_g, ln_b))
    return h @ w_o


def spatial_gating(z, ln_g, ln_b, w_s, b_s, w_o):
    z = jax.nn.gelu(z, approximate=False)
    u, v = jnp.split(z, 2, axis=-1)
    v = layer_norm(v, ln_g, ln_b)
    b_, s_, _ = v.shape
    v = v.reshape(b_, s_ // CHUNK, CHUNK, SGU_GROUPS, SGU_GROUP_DIM)
    causal = jnp.tril(jnp.ones((CHUNK, CHUNK), dtype=bool))
    w = jnp.where(causal[None], w_s, 0)
    v = jnp.einsum('gts,bcsgd->bctgd', w, v) + b_s.T[None, None, :, :, None]
    return (u * v.reshape(b_, s_, SGU_WIDTH)) @ w_o


def memory_cross_attention(q, mem_n, w_kv, w_o):
    b_, s_, _ = q.shape
    k, v = jnp.split(mem_n @ w_kv, 2, axis=-1)
    k = k.reshape(b_, -1, XATTN_HEADS, XATTN_HEAD_DIM)
    v = v.reshape(b_, -1, XATTN_HEADS, XATTN_HEAD_DIM)
    q = q.reshape(b_, s_, XATTN_HEADS, XATTN_HEAD_DIM)
    s = jnp.einsum('bshd,bmhd->bhsm', q, k).astype(jnp.float32) * (XATTN_HEAD_DIM ** -0.5)
    p = jax.nn.softmax(s, axis=-1).astype(v.dtype)
    o = jnp.einsum('bhsm,bmhd->bshd', p, v).reshape(b_, s_, XATTN_WIDTH)
    return o @ w_o


def moe_ffn(n, w_router, b_router, w_gate_up, b_gate_up, w_down, b_down):
    t_, d_ = n.shape
    logits = (n @ w_router).astype(jnp.float32) + b_router.astype(jnp.float32)
    top_val, top_idx = lax.top_k(logits, TOP_K)
    probs = jax.nn.softmax(top_val, axis=-1)
    tk = t_ * TOP_K
    flat_e = top_idx.reshape(-1)
    order = jnp.argsort(flat_e)
    sorted_e = flat_e[order]
    tok = order // TOP_K
    w_sorted = probs.reshape(-1)[order]
    counts = jnp.bincount(flat_e, length=N_EXPERTS)
    start = jnp.cumsum(counts) - counts
    padded = ((counts + EXPERT_BLOCK - 1) // EXPERT_BLOCK) * EXPERT_BLOCK
    pend = jnp.cumsum(padded)
    pstart = pend - padded
    dest = pstart[sorted_e] + (jnp.arange(tk) - start[sorted_e])
    n_blocks = -(-tk // EXPERT_BLOCK) + N_EXPERTS
    cap = n_blocks * EXPERT_BLOCK
    xp = jnp.zeros((cap, d_), n.dtype).at[dest].set(n[tok])
    block_e = jnp.clip(jnp.searchsorted(pend, jnp.arange(n_blocks) * EXPERT_BLOCK, side='right'),
                       0, N_EXPERTS - 1)

    def expert_block(args):
        xb, e = args
        gu = xb @ w_gate_up[e] + b_gate_up[e]
        g, u = jnp.split(gu, 2, axis=-1)
        g = jnp.minimum(g, SWIGLU_LIMIT)
        u = jnp.clip(u, -SWIGLU_LIMIT, SWIGLU_LIMIT)
        h = (u + 1) * (g * jax.nn.sigmoid(SWIGLU_ALPHA * g))
        return h @ w_down[e] + b_down[e]

    yp = lax.map(expert_block, (xp.reshape(n_blocks, EXPERT_BLOCK, d_), block_e)).reshape(cap, d_)
    y = jnp.zeros((t_, d_), n.dtype).at[tok].add(yp[dest] * w_sorted[:, None].astype(n.dtype))
    return y


def setup_inputs(seed: int = 0) -> dict:
    key = jax.random.key(seed)
    ks = jax.random.split(key, 32)

    def nrm(k, shape, scale):
        return jax.random.normal(k, shape, jnp.float32) * scale

    L = DEPTH
    return {
        'x': nrm(ks[0], (BATCH, SEQ, D_MODEL), 1.0),
        'mem': nrm(ks[1], (BATCH, N_MEM, D_MODEL), 1.0),
        'norm1_g': 1.0 + nrm(ks[2], (L, D_MODEL), 0.02),
        'w_in': nrm(ks[3], (L, D_MODEL, IN_COLS), D_MODEL ** -0.5),
        'b_gate': nrm(ks[4], (L, N_BRANCH * D_MODEL), 0.02),
        'w_dw': nrm(ks[5], (L, CONV_KERNEL, CONV_CH), CONV_KERNEL ** -0.5),
        'b_dw': nrm(ks[6], (L, CONV_CH), 0.02),
        'conv_ln_g': 1.0 + nrm(ks[7], (L, CONV_CH), 0.02),
        'conv_ln_b': nrm(ks[8], (L, CONV_CH), 0.02),
        'w_conv_out': nrm(ks[9], (L, CONV_CH, D_MODEL), CONV_CH ** -0.5),
        'sgu_ln_g': 1.0 + nrm(ks[10], (L, SGU_WIDTH), 0.02),
        'sgu_ln_b': nrm(ks[11], (L, SGU_WIDTH), 0.02),
        'w_spatial': nrm(ks[12], (L, SGU_GROUPS, CHUNK, CHUNK), CHUNK ** -0.5),
        'b_spatial': 1.0 + nrm(ks[13], (L, SGU_GROUPS, CHUNK), 0.02),
        'w_sgu_out': nrm(ks[14], (L, SGU_WIDTH, D_MODEL), SGU_WIDTH ** -0.5),
        'mem_norm_g': 1.0 + nrm(ks[15], (L, D_MODEL), 0.02),
        'w_kv': nrm(ks[16], (L, D_MODEL, 2 * XATTN_WIDTH), D_MODEL ** -0.5),
        'w_xattn_out': nrm(ks[17], (L, XATTN_WIDTH, D_MODEL), XATTN_WIDTH ** -0.5),
        'w_out': nrm(ks[18], (L, D_MODEL, D_MODEL), D_MODEL ** -0.5),
        'norm2_g': 1.0 + nrm(ks[19], (L, D_MODEL), 0.02),
        'w_router': nrm(ks[20], (L, D_MODEL, N_EXPERTS), D_MODEL ** -0.5),
        'b_router': nrm(ks[21], (L, N_EXPERTS), 0.01),
        'w_gate_up': nrm(ks[22], (L, N_EXPERTS, D_MODEL, 2 * D_EXPERT), D_MODEL ** -0.5),
        'b_gate_up': nrm(ks[23], (L, N_EXPERTS, 2 * D_EXPERT), 0.02),
        'w_down': nrm(ks[24], (L, N_EXPERTS, D_EXPERT, D_MODEL), D_EXPERT ** -0.5),
        'b_down': nrm(ks[25], (L, N_EXPERTS, D_MODEL), 0.02),
        'final_norm_g': 1.0 + nrm(ks[26], (D_MODEL,), 0.02),
    }


def reference(x, mem, norm1_g, w_in, b_gate, w_dw, b_dw, conv_ln_g, conv_ln_b, w_conv_out,
              sgu_ln_g, sgu_ln_b, w_spatial, b_spatial, w_sgu_out, mem_norm_g, w_kv,
              w_xattn_out, w_out, norm2_g, w_router, b_router, w_gate_up, b_gate_up,
              w_down, b_down, final_norm_g):
    b_, s_, d_ = x.shape
    split_at = [2 * CONV_CH, 2 * CONV_CH + 2 * SGU_WIDTH,
                2 * CONV_CH + 2 * SGU_WIDTH + XATTN_WIDTH]
    h = x
    for l in range(DEPTH):
        n = rms_norm(h, norm1_g[l])
        proj = n @ w_in[l]
        a_conv, z_sgu, q_x, g_logit = jnp.split(proj, split_at, axis=-1)
        y_conv = conformer_conv(a_conv, w_dw[l], b_dw[l], conv_ln_g[l], conv_ln_b[l], w_conv_out[l])
        y_sgu = spatial_gating(z_sgu, sgu_ln_g[l], sgu_ln_b[l], w_spatial[l], b_spatial[l], w_sgu_out[l])
        mem_n = rms_norm(mem, mem_norm_g[l])
        y_x = memory_cross_attention(q_x, mem_n, w_kv[l], w_xattn_out[l])
        gates = jax.nn.sigmoid(g_logit + b_gate[l]).reshape(b_, s_, N_BRANCH, d_)
        merged = gates[:, :, 0] * y_conv + gates[:, :, 1] * y_sgu + gates[:, :, 2] * y_x
        h = h + merged @ w_out[l]
        n2 = rms_norm(h, norm2_g[l]).reshape(b_ * s_, d_)
        y_moe = moe_ffn(n2, w_router[l], b_router[l], w_gate_up[l], b_gate_up[l], w_down[l], b_down[l])
        h = h + y_moe.reshape(b_, s_, d_)
    return rms_norm(h, final_norm_g)
```

```python
import functools
import math

import jax
import jax.numpy as jnp
from jax import lax
from jax.experimental import pallas as pl
from jax.experimental.pallas import tpu as pltpu

EPS = 1e-5
TOP_K = 4
SWIGLU_LIMIT = 7.0
SWIGLU_ALPHA = 1.702
XATTN_HEAD_DIM_DIVISOR = 8

V7X_LANES = 128
V7X_SUBLANES = 8
V7X_VMEM_LIMIT_BYTES = 56 * 1024 * 1024

F32 = jnp.float32
BF16 = jnp.bfloat16


def _params(*sem):
    return pltpu.CompilerParams(dimension_semantics=sem, vmem_limit_bytes=V7X_VMEM_LIMIT_BYTES)


def _sigmoid(x):
    return 1.0 / (1.0 + jnp.exp(-x))


def _layer_norm(x, g, b):
    mu = jnp.mean(x, axis=-1, keepdims=True)
    xc = x - mu
    var = jnp.mean(xc * xc, axis=-1, keepdims=True)
    return xc * lax.rsqrt(var + EPS) * g + b


def _rms_norm(x, g):
    return x * lax.rsqrt(jnp.mean(x * x, axis=-1, keepdims=True) + EPS) * g


def _norm_matmul_kernel(x_ref, g_ref, w_ref, o_ref, xn_ref):
    @pl.when(pl.program_id(1) == 0)
    def _():
        xn_ref[...] = _rms_norm(x_ref[...], g_ref[...]).astype(BF16)

    o_ref[...] = jnp.dot(xn_ref[...], w_ref[...], preferred_element_type=F32).astype(o_ref.dtype)


def _norm_matmul(x2d, g, w_bf16, tm, tn):
    m, d = x2d.shape
    n = w_bf16.shape[1]
    return pl.pallas_call(
        _norm_matmul_kernel,
        grid=(m // tm, n // tn),
        in_specs=[
            pl.BlockSpec((tm, d), lambda i, j: (i, 0)),
            pl.BlockSpec((1, d), lambda i, j: (0, 0)),
            pl.BlockSpec((d, tn), lambda i, j: (0, j)),
        ],
        out_specs=pl.BlockSpec((tm, tn), lambda i, j: (i, j)),
        out_shape=jax.ShapeDtypeStruct((m, n), BF16),
        scratch_shapes=[pltpu.VMEM((tm, d), BF16)],
        compiler_params=_params("arbitrary", "arbitrary"),
        name="norm_matmul",
    )(x2d, g.reshape(1, d), w_bf16)


CONV_HALO = 32
CONV_ROWS = 64
CONV_LANES = 256


def _conv_kernel(p_ref, q_ref, gl_ref, bg_ref, wdw_ref, bdw_ref, lg_ref, lb_ref, wo_ref, o_ref,
                 cp_ref, halo_ref, conv_ref, act_ref, *, tm, tiles_per_seq, kw):
    cc = p_ref.shape[1]

    @pl.when(pl.program_id(1) == 0)
    def _():
        @pl.when(pl.program_id(0) % tiles_per_seq == 0)
        def _():
            halo_ref[...] = jnp.zeros_like(halo_ref)

        hg = p_ref[...].astype(F32) * _sigmoid(q_ref[...].astype(F32))
        halo = halo_ref[...]
        for r in range(V7X_SUBLANES):
            cp_ref[r, pl.ds(r, CONV_HALO), :] = halo
            cp_ref[r, pl.ds(CONV_HALO + r, tm), :] = hg
        halo_ref[...] = hg[tm - CONV_HALO:, :]

        for lc in range(cc // CONV_LANES):
            lanes = slice(lc * CONV_LANES, (lc + 1) * CONV_LANES)
            bias = bdw_ref[:, lanes]

            def body(rc, carry, lanes=lanes, bias=bias):
                base = rc * CONV_ROWS
                acc = jnp.zeros((CONV_ROWS // V7X_SUBLANES, V7X_SUBLANES, CONV_LANES), F32)
                for k in range(kw):
                    delay = kw - 1 - k
                    r = delay % V7X_SUBLANES
                    start = pl.multiple_of(base + (CONV_HALO - (delay - r)), V7X_SUBLANES)
                    win = cp_ref[r, pl.ds(start, CONV_ROWS), lanes]
                    win = win.reshape(CONV_ROWS // V7X_SUBLANES, V7X_SUBLANES, CONV_LANES)
                    acc = acc + win * wdw_ref[k, :, lanes][None]
                conv_ref[pl.ds(pl.multiple_of(base, CONV_ROWS), CONV_ROWS), lanes] = (
                    acc.reshape(CONV_ROWS, CONV_LANES) + bias)
                return carry

            lax.fori_loop(0, tm // CONV_ROWS, body, 0)

        y = _layer_norm(conv_ref[...], lg_ref[...], lb_ref[...])
        act_ref[...] = (y * _sigmoid(y)).astype(BF16)

    yo = jnp.dot(act_ref[...], wo_ref[...], preferred_element_type=F32)
    gate = _sigmoid(gl_ref[...].astype(F32) + bg_ref[...])
    o_ref[...] = (gate * yo).astype(o_ref.dtype)


def _gelu(x):
    return 0.5 * x * (1.0 + lax.erf(x * (1.0 / math.sqrt(2.0))))


def _sgu_kernel(u_ref, v_ref, gl_ref, bg_ref, acc_ref, lg_ref, lb_ref, ws_ref, bs_ref, wo_ref, o_ref,
                s_ref, *, tm, groups, chunk):
    sw = u_ref.shape[1]
    gd = sw // groups
    nchunk = tm // chunk

    @pl.when(pl.program_id(1) == 0)
    def _():
        u = _gelu(u_ref[...].astype(F32))
        v = _gelu(v_ref[...].astype(F32))
        vb = _layer_norm(v, lg_ref[...], lb_ref[...]).astype(BF16)
        row = lax.broadcasted_iota(jnp.int32, (chunk, chunk), 0)
        col = lax.broadcasted_iota(jnp.int32, (chunk, chunk), 1)
        causal = row >= col
        for g in range(groups):
            lanes = slice(g * gd, (g + 1) * gd)
            wg = jnp.where(causal, ws_ref[g], 0.0).astype(BF16)
            vcat = jnp.concatenate([vb[c * chunk:(c + 1) * chunk, lanes] for c in range(nchunk)], axis=1)
            og = jnp.dot(wg, vcat, preferred_element_type=F32)
            for c in range(nchunk):
                rows = slice(c * chunk, (c + 1) * chunk)
                sv = og[:, c * gd:(c + 1) * gd] + bs_ref[:, lanes]
                s_ref[rows, lanes] = (u[rows, lanes] * sv).astype(BF16)

    yo = jnp.dot(s_ref[...], wo_ref[...], preferred_element_type=F32)
    gate = _sigmoid(gl_ref[...].astype(F32) + bg_ref[...])
    o_ref[...] = (acc_ref[...].astype(F32) + gate * yo).astype(o_ref.dtype)


def _xattn_kernel(q_ref, k_ref, v_ref, gl_ref, bg_ref, acc_ref, wo_ref, o_ref, att_ref, *, heads):
    xw = q_ref.shape[1]
    hd = xw // heads

    @pl.when(pl.program_id(1) == 0)
    def _():
        scale = hd ** -0.5
        for h in range(heads):
            lanes = slice(h * hd, (h + 1) * hd)
            s = lax.dot_general(q_ref[:, lanes], k_ref[:, lanes], (((1,), (1,)), ((), ())),
                                preferred_element_type=F32) * scale
            s = s - jnp.max(s, axis=-1, keepdims=True)
            e = jnp.exp(s)
            p = e / jnp.sum(e, axis=-1, keepdims=True)
            oh = jnp.dot(p.astype(BF16), v_ref[:, lanes], preferred_element_type=F32)
            att_ref[:, lanes] = oh.astype(BF16)

    yo = jnp.dot(att_ref[...], wo_ref[...], preferred_element_type=F32)
    gate = _sigmoid(gl_ref[...].astype(F32) + bg_ref[...])
    o_ref[...] = (acc_ref[...].astype(F32) + gate * yo).astype(o_ref.dtype)


def _pack_bf16_pairs(x):
    half = x.shape[1] // 2
    lo = lax.bitcast_convert_type(x[:, :half].astype(BF16).astype(F32), jnp.uint32)
    hi = lax.bitcast_convert_type(x[:, half:].astype(BF16).astype(F32), jnp.uint32)
    return (hi & jnp.uint32(0xFFFF0000)) | (lo >> 16)


def _unpack_bf16_pairs(w):
    lo = lax.bitcast_convert_type(w << 16, F32)
    hi = lax.bitcast_convert_type(w & jnp.uint32(0xFFFF0000), F32)
    return jnp.concatenate([lo, hi], axis=1).astype(BF16)


def _mix_route_kernel(m_ref, x_ref, wo_ref, g_ref, wrh_ref, wrl_ref, br_ref,
                      h_ref, n2_ref, idx_ref, prob_ref, rank_ref, cnt_ref, carry_ref, *, experts):
    tm = x_ref.shape[0]

    @pl.when(pl.program_id(0) == 0)
    def _():
        carry_ref[...] = jnp.zeros_like(carry_ref)

    h = x_ref[...] + jnp.dot(m_ref[...], wo_ref[...], preferred_element_type=F32)
    h_ref[...] = h
    n2 = _rms_norm(h, g_ref[...])
    n2_ref[...] = _pack_bf16_pairs(n2)

    n_hi = n2.astype(BF16)
    n_lo = (n2 - n_hi.astype(F32)).astype(BF16)
    nt = (((1,), (1,)), ((), ()))
    lg = (lax.dot_general(wrh_ref[...], n_hi, nt, preferred_element_type=F32)
          + lax.dot_general(wrh_ref[...], n_lo, nt, preferred_element_type=F32)
          + lax.dot_general(wrl_ref[...], n_hi, nt, preferred_element_type=F32)
          + br_ref[...])

    eidx = lax.broadcasted_iota(jnp.int32, (experts, tm), 0)
    vals, onehots = [], []
    for k in range(TOP_K):
        mx = jnp.max(lg, axis=0, keepdims=True)
        ik = jnp.min(jnp.where(lg == mx, eidx, experts), axis=0, keepdims=True)
        sel = eidx == ik
        vals.append(mx)
        onehots.append(sel)
        idx_ref[k:k + 1, :] = ik
        lg = jnp.where(sel, -jnp.inf, lg)

    exps = [jnp.exp(v - vals[0]) for v in vals]
    denom = exps[0] + exps[1] + exps[2] + exps[3]
    for k in range(TOP_K):
        prob_ref[k:k + 1, :] = exps[k] / denom

    chosen = jnp.zeros((experts, tm), F32)
    for sel in onehots:
        chosen = chosen + sel.astype(F32)
    src = lax.broadcasted_iota(jnp.int32, (tm, tm), 0)
    dst = lax.broadcasted_iota(jnp.int32, (tm, tm), 1)
    before = (src < dst).astype(BF16)
    prior = jnp.dot(chosen.astype(BF16), before, preferred_element_type=F32) + carry_ref[...]
    for k in range(TOP_K):
        rk = jnp.sum(jnp.where(onehots[k], prior, 0.0), axis=0, keepdims=True)
        rank_ref[k:k + 1, :] = rk.astype(jnp.int32)
    carry_ref[...] = carry_ref[...] + jnp.sum(chosen, axis=1, keepdims=True)
    cnt_ref[...] = jnp.broadcast_to(carry_ref[...], cnt_ref.shape).astype(jnp.int32)


def _dispatch_kernel(dest_ref, n2_ref, xp_in_ref, xp_ref, sem):
    del xp_in_ref
    tm = n2_ref.shape[0]

    def row_copy(t, d):
        return pltpu.make_async_copy(n2_ref.at[pl.ds(t, 1)], xp_ref.at[pl.ds(d, 1)], sem)

    def issue(t, carry):
        for k in range(TOP_K):
            row_copy(t, dest_ref[k, t]).start()
        return carry

    lax.fori_loop(0, tm, issue, 0)

    def drain(t, carry):
        for k in range(TOP_K):
            row_copy(t, dest_ref[k, t]).wait()
        return carry

    lax.fori_loop(0, tm, drain, 0)


def _expert_up_kernel(be_ref, nu_ref, x_ref, wg_ref, wu_ref, bg_ref, bu_ref, o_ref, wgb_ref, wub_ref):
    m = pl.program_id(1)

    @pl.when(m < nu_ref[0])
    def _():
        prev = be_ref[jnp.maximum(m - 1, 0)]

        @pl.when((m == 0) | (be_ref[m] != prev))
        def _():
            wgb_ref[...] = wg_ref[0].astype(BF16)
            wub_ref[...] = wu_ref[0].astype(BF16)

        x = _unpack_bf16_pairs(x_ref[...])
        g = jnp.dot(x, wgb_ref[...], preferred_element_type=F32) + bg_ref[0]
        u = jnp.dot(x, wub_ref[...], preferred_element_type=F32) + bu_ref[0]
        g = jnp.minimum(g, SWIGLU_LIMIT)
        u = jnp.clip(u, -SWIGLU_LIMIT, SWIGLU_LIMIT)
        o_ref[...] = ((u + 1.0) * (g * _sigmoid(SWIGLU_ALPHA * g))).astype(o_ref.dtype)

    @pl.when(m >= nu_ref[0])
    def _():
        o_ref[...] = jnp.zeros_like(o_ref)


def _expert_down_kernel(be_ref, nu_ref, h_ref, w_ref, b_ref, o_ref, wb_ref):
    m = pl.program_id(1)

    @pl.when(m < nu_ref[0])
    def _():
        prev = be_ref[jnp.maximum(m - 1, 0)]

        @pl.when((m == 0) | (be_ref[m] != prev))
        def _():
            wb_ref[...] = w_ref[0].astype(BF16)

        o_ref[...] = jnp.dot(h_ref[...], wb_ref[...], preferred_element_type=F32) + b_ref[0]

    @pl.when(m >= nu_ref[0])
    def _():
        o_ref[...] = jnp.zeros_like(o_ref)


def _combine_kernel(dest_ref, p_ref, h_ref, g_ref, yp_ref, o_ref, rows_ref, sem):
    tm = h_ref.shape[0]

    def row_copy(t, k):
        return pltpu.make_async_copy(yp_ref.at[pl.ds(dest_ref[k, t], 1)], rows_ref.at[k, pl.ds(t, 1)], sem)

    def issue(t, carry):
        for k in range(TOP_K):
            row_copy(t, k).start()
        return carry

    lax.fori_loop(0, tm, issue, 0)

    def drain(t, carry):
        for k in range(TOP_K):
            row_copy(t, k).wait()
        return carry

    lax.fori_loop(0, tm, drain, 0)

    y = h_ref[...]
    for k in range(TOP_K):
        y = y + p_ref[:, k:k + 1] * rows_ref[k]
    o_ref[...] = _rms_norm(y, g_ref[...])


def kernel(x, mem, norm1_g, w_in, b_gate, w_dw, b_dw, conv_ln_g, conv_ln_b, w_conv_out, sgu_ln_g, sgu_ln_b,
           w_spatial, b_spatial, w_sgu_out, mem_norm_g, w_kv, w_xattn_out, w_out, norm2_g, w_router, b_router,
           w_gate_up, b_gate_up, w_down, b_down, final_norm_g):
    b, s, d = x.shape
    depth = norm1_g.shape[0]
    t = b * s
    nm = mem.shape[1]
    kw, cc = w_dw.shape[1], w_dw.shape[2]
    sw = w_sgu_out.shape[1]
    groups, chunk = w_spatial.shape[1], w_spatial.shape[2]
    xw = w_xattn_out.shape[1]
    heads = xw // (d // XATTN_HEAD_DIM_DIVISOR)
    experts = w_router.shape[2]
    de = w_down.shape[2]
    in_cols = w_in.shape[2]

    unit = math.gcd(math.gcd(cc, sw), math.gcd(xw, d))
    assert in_cols == 2 * cc + 2 * sw + xw + 3 * d and in_cols % unit == 0
    assert (sw // groups) % V7X_LANES == 0 and kw - 1 <= CONV_HALO and cc % CONV_LANES == 0
    assert (2 * cc) % sw == 0 and (2 * cc + 2 * sw) % xw == 0 and d % 2 == 0

    tm = min(512, s)
    assert s % tm == 0 and tm % chunk == 0 and tm % CONV_ROWS == 0
    tiles_per_seq = s // tm
    nj = d // unit
    gate_col0 = (2 * cc + 2 * sw + xw) // unit

    h = x.reshape(t, d)
    for l in range(depth):
        proj = _norm_matmul(h, norm1_g[l], w_in[l].astype(BF16), tm, unit)
        kv = _norm_matmul(mem.reshape(b * nm, d), mem_norm_g[l], w_kv[l].astype(BF16), min(512, b * nm), xw)
        bg = b_gate[l].reshape(1, 3 * d)

        def gate_specs(branch):
            return [pl.BlockSpec((tm, unit), lambda i, j: (i, gate_col0 + branch * nj + j)),
                    pl.BlockSpec((1, unit), lambda i, j: (0, branch * nj + j))]

        tile_spec = pl.BlockSpec((tm, unit), lambda i, j: (i, j))
        row_vec = lambda n: pl.BlockSpec((1, n), lambda i, j: (0, 0))

        wdw_b = jnp.broadcast_to(w_dw[l][:, None, :], (kw, V7X_SUBLANES, cc))
        gy = pl.pallas_call(
            functools.partial(_conv_kernel, tm=tm, tiles_per_seq=tiles_per_seq, kw=kw),
            grid=(t // tm, nj),
            in_specs=[pl.BlockSpec((tm, cc), lambda i, j: (i, 0)),
                      pl.BlockSpec((tm, cc), lambda i, j: (i, 1)),
                      *gate_specs(0),
                      pl.BlockSpec((kw, V7X_SUBLANES, cc), lambda i, j: (0, 0, 0)),
                      row_vec(cc), row_vec(cc), row_vec(cc),
                      pl.BlockSpec((cc, unit), lambda i, j: (0, j))],
            out_specs=tile_spec,
            out_shape=jax.ShapeDtypeStruct((t, d), BF16),
            scratch_shapes=[pltpu.VMEM((V7X_SUBLANES, tm + CONV_HALO + V7X_SUBLANES, cc), F32),
                            pltpu.VMEM((CONV_HALO, cc), F32),
                            pltpu.VMEM((tm, cc), F32),
                            pltpu.VMEM((tm, cc), BF16)],
            compiler_params=_params("arbitrary", "arbitrary"),
            name="conv_branch",
        )(proj, proj, proj, bg, wdw_b, b_dw[l].reshape(1, cc), conv_ln_g[l].reshape(1, cc),
          conv_ln_b[l].reshape(1, cc), w_conv_out[l].astype(BF16))

        bs_full = jnp.repeat(b_spatial[l].T, sw // groups, axis=1)
        gy = pl.pallas_call(
            functools.partial(_sgu_kernel, tm=tm, groups=groups, chunk=chunk),
            grid=(t // tm, nj),
            in_specs=[pl.BlockSpec((tm, sw), lambda i, j: (i, 2 * cc // sw)),
                      pl.BlockSpec((tm, sw), lambda i, j: (i, 2 * cc // sw + 1)),
                      *gate_specs(1),
                      tile_spec,
                      row_vec(sw), row_vec(sw),
                      pl.BlockSpec((groups, chunk, chunk), lambda i, j: (0, 0, 0)),
                      pl.BlockSpec((chunk, sw), lambda i, j: (0, 0)),
                      pl.BlockSpec((sw, unit), lambda i, j: (0, j))],
            out_specs=tile_spec,
            out_shape=jax.ShapeDtypeStruct((t, d), BF16),
            scratch_shapes=[pltpu.VMEM((tm, sw), BF16)],
            compiler_params=_params("arbitrary", "arbitrary"),
            name="sgu_branch",
        )(proj, proj, proj, bg, gy, sgu_ln_g[l].reshape(1, sw), sgu_ln_b[l].reshape(1, sw),
          w_spatial[l], bs_full, w_sgu_out[l].astype(BF16))

        merged = pl.pallas_call(
            functools.partial(_xattn_kernel, heads=heads),
            grid=(t // tm, nj),
            in_specs=[pl.BlockSpec((tm, xw), lambda i, j: (i, (2 * cc + 2 * sw) // xw)),
                      pl.BlockSpec((nm, xw), lambda i, j: (i // tiles_per_seq, 0)),
                      pl.BlockSpec((nm, xw), lambda i, j: (i // tiles_per_seq, 1)),
                      *gate_specs(2),
                      tile_spec,
                      pl.BlockSpec((xw, unit), lambda i, j: (0, j))],
            out_specs=tile_spec,
            out_shape=jax.ShapeDtypeStruct((t, d), BF16),
            scratch_shapes=[pltpu.VMEM((tm, xw), BF16)],
            compiler_params=_params("arbitrary", "arbitrary"),
            name="xattn_branch",
        )(proj, kv, kv, proj, bg, gy, w_xattn_out[l].astype(BF16))

        wr = w_router[l].T
        wr_hi = wr.astype(BF16)
        wr_lo = (wr - wr_hi.astype(F32)).astype(BF16)
        tr = min(512, t)
        full = lambda shape: pl.BlockSpec(shape, lambda i: tuple(0 for _ in shape))
        kt_spec = pl.BlockSpec((TOP_K, tr), lambda i: (0, i))
        h, n2p, top_idx, probs, rank, counts = pl.pallas_call(
            functools.partial(_mix_route_kernel, experts=experts),
            grid=(t // tr,),
            in_specs=[pl.BlockSpec((tr, d), lambda i: (i, 0)),
                      pl.BlockSpec((tr, d), lambda i: (i, 0)),
                      full((d, d)), full((1, d)), full((experts, d)), full((experts, d)), full((experts, 1))],
            out_specs=[pl.BlockSpec((tr, d), lambda i: (i, 0)),
                       pl.BlockSpec((tr, d // 2), lambda i: (i, 0)),
                       kt_spec, kt_spec, kt_spec,
                       full((experts, V7X_LANES))],
            out_shape=[jax.ShapeDtypeStruct((t, d), F32),
                       jax.ShapeDtypeStruct((t, d // 2), jnp.uint32),
                       jax.ShapeDtypeStruct((TOP_K, t), jnp.int32),
                       jax.ShapeDtypeStruct((TOP_K, t), F32),
                       jax.ShapeDtypeStruct((TOP_K, t), jnp.int32),
                       jax.ShapeDtypeStruct((experts, V7X_LANES), jnp.int32)],
            scratch_shapes=[pltpu.VMEM((experts, 1), F32)],
            compiler_params=_params("arbitrary"),
            name="mix_route",
        )(merged, h, w_out[l].astype(BF16), norm2_g[l].reshape(1, d), wr_hi, wr_lo, b_router[l].reshape(experts, 1))

        te = 256
        counts = counts[:, 0]
        padded = ((counts + te - 1) // te) * te
        pend = jnp.cumsum(padded)
        pstart = pend - padded
        dest = pstart[top_idx] + rank
        n_blocks = (t * TOP_K) // te + experts
        cap = n_blocks * te
        block_e = jnp.clip(jnp.searchsorted(pend, jnp.arange(n_blocks, dtype=jnp.int32) * te, side="right"),
                           0, experts - 1).astype(jnp.int32)
        n_used = (pend[-1:] // te).astype(jnp.int32)

        td = min(256, t)
        xp = pl.pallas_call(
            _dispatch_kernel,
            grid=(t // td,),
            in_specs=[pl.BlockSpec((TOP_K, td), lambda i: (0, i), memory_space=pltpu.SMEM),
                      pl.BlockSpec((td, d // 2), lambda i: (i, 0)),
                      pl.BlockSpec(memory_space=pl.ANY)],
            out_specs=pl.BlockSpec(memory_space=pl.ANY),
            out_shape=jax.ShapeDtypeStruct((cap, d // 2), jnp.uint32),
            scratch_shapes=[pltpu.SemaphoreType.DMA(())],
            input_output_aliases={2: 0},
            compiler_params=_params("arbitrary"),
            name="dispatch",
        )(dest, n2p, jnp.zeros((cap, d // 2), jnp.uint32))

        tn_up = min(512, de)
        n_up = de // tn_up
        clamp = lambda m, nu: jnp.minimum(m, nu[0] - 1)
        hidden = pl.pallas_call(
            _expert_up_kernel,
            grid_spec=pltpu.PrefetchScalarGridSpec(
                num_scalar_prefetch=2,
                grid=(n_up, n_blocks),
                in_specs=[pl.BlockSpec((te, d // 2), lambda p, m, be, nu: (clamp(m, nu), 0)),
                          pl.BlockSpec((1, d, tn_up), lambda p, m, be, nu: (be[clamp(m, nu)], 0, p)),
                          pl.BlockSpec((1, d, tn_up), lambda p, m, be, nu: (be[clamp(m, nu)], 0, n_up + p)),
                          pl.BlockSpec((1, 1, tn_up), lambda p, m, be, nu: (be[clamp(m, nu)], 0, p)),
                          pl.BlockSpec((1, 1, tn_up), lambda p, m, be, nu: (be[clamp(m, nu)], 0, n_up + p))],
                out_specs=pl.BlockSpec((te, tn_up), lambda p, m, be, nu: (m, p)),
                scratch_shapes=[pltpu.VMEM((d, tn_up), BF16), pltpu.VMEM((d, tn_up), BF16)]),
            out_shape=jax.ShapeDtypeStruct((cap, de), BF16),
            compiler_params=_params("arbitrary", "arbitrary"),
            name="expert_up",
        )(block_e, n_used, xp, w_gate_up[l], w_gate_up[l], b_gate_up[l].reshape(experts, 1, 2 * de),
          b_gate_up[l].reshape(experts, 1, 2 * de))

        tn_dn = min(1024, d)
        n_dn = d // tn_dn
        yp = pl.pallas_call(
            _expert_down_kernel,
            grid_spec=pltpu.PrefetchScalarGridSpec(
                num_scalar_prefetch=2,
                grid=(n_dn, n_blocks),
                in_specs=[pl.BlockSpec((te, de), lambda p, m, be, nu: (clamp(m, nu), 0)),
                          pl.BlockSpec((1, de, tn_dn), lambda p, m, be, nu: (be[clamp(m, nu)], 0, p)),
                          pl.BlockSpec((1, 1, tn_dn), lambda p, m, be, nu: (be[clamp(m, nu)], 0, p))],
                out_specs=pl.BlockSpec((te, tn_dn), lambda p, m, be, nu: (m, p)),
                scratch_shapes=[pltpu.VMEM((de, tn_dn), BF16)]),
            out_shape=jax.ShapeDtypeStruct((cap, d), F32),
            compiler_params=_params("arbitrary", "arbitrary"),
            name="expert_down",
        )(block_e, n_used, hidden, w_down[l], b_down[l].reshape(experts, 1, d))

        assert depth == 1
        tc = min(256, t)
        h = pl.pallas_call(
            _combine_kernel,
            grid=(t // tc,),
            in_specs=[pl.BlockSpec((TOP_K, tc), lambda i: (0, i), memory_space=pltpu.SMEM),
                      pl.BlockSpec((tc, TOP_K), lambda i: (i, 0)),
                      pl.BlockSpec((tc, d), lambda i: (i, 0)),
                      pl.BlockSpec((1, d), lambda i: (0, 0)),
                      pl.BlockSpec(memory_space=pl.ANY)],
            out_specs=pl.BlockSpec((tc, d), lambda i: (i, 0)),
            out_shape=jax.ShapeDtypeStruct((t, d), F32),
            scratch_shapes=[pltpu.VMEM((TOP_K, tc, d), F32), pltpu.SemaphoreType.DMA(())],
            compiler_params=_params("arbitrary"),
            name="combine",
        )(dest, probs.T, h, final_norm_g.reshape(1, d), yp)

    return h.reshape(b, s, d)
```

```python
import functools
import math

import jax
import jax.numpy as jnp
from jax import lax
from jax.experimental import pallas as pl
from jax.experimental.pallas import tpu as pltpu

EPS = 1e-5
TOP_K = 4
SWIGLU_LIMIT = 7.0
SWIGLU_ALPHA = 1.702
XATTN_HEAD_DIM_DIVISOR = 8

V7X_LANES = 128
V7X_SUBLANES = 8
V7X_VMEM_LIMIT_BYTES = 56 * 1024 * 1024

F32 = jnp.float32
BF16 = jnp.bfloat16


def _params(*sem):
    return pltpu.CompilerParams(dimension_semantics=sem, vmem_limit_bytes=V7X_VMEM_LIMIT_BYTES)


def _sigmoid(x):
    return 1.0 / (1.0 + jnp.exp(-x))


def _layer_norm(x, g, b):
    mu = jnp.mean(x, axis=-1, keepdims=True)
    xc = x - mu
    var = jnp.mean(xc * xc, axis=-1, keepdims=True)
    return xc * lax.rsqrt(var + EPS) * g + b


def _rms_norm(x, g):
    return x * lax.rsqrt(jnp.mean(x * x, axis=-1, keepdims=True) + EPS) * g


def _norm_matmul_kernel(x_ref, g_ref, w_ref, o_ref, xn_ref):
    @pl.when(pl.program_id(1) == 0)
    def _():
        xn_ref[...] = _rms_norm(x_ref[...], g_ref[...]).astype(BF16)

    o_ref[...] = jnp.dot(xn_ref[...], w_ref[...], preferred_element_type=F32).astype(o_ref.dtype)


def _norm_matmul(x2d, g, w_bf16, tm, tn):
    m, d = x2d.shape
    n = w_bf16.shape[1]
    return pl.pallas_call(
        _norm_matmul_kernel,
        grid=(m // tm, n // tn),
        in_specs=[
            pl.BlockSpec((tm, d), lambda i, j: (i, 0)),
            pl.BlockSpec((1, d), lambda i, j: (0, 0)),
            pl.BlockSpec((d, tn), lambda i, j: (0, j)),
        ],
        out_specs=pl.BlockSpec((tm, tn), lambda i, j: (i, j)),
        out_shape=jax.ShapeDtypeStruct((m, n), BF16),
        scratch_shapes=[pltpu.VMEM((tm, d), BF16)],
        compiler_params=_params("arbitrary", "arbitrary"),
        name="norm_matmul",
    )(x2d, g.reshape(1, d), w_bf16)


CONV_HALO = 32
CONV_ROWS = 64
CONV_LANES = 256


def _conv_kernel(p_ref, q_ref, gl_ref, bg_ref, wdw_ref, bdw_ref, lg_ref, lb_ref, wo_ref, o_ref,
                 cp_ref, halo_ref, conv_ref, act_ref, *, tm, tiles_per_seq, kw):
    cc = p_ref.shape[1]

    @pl.when(pl.program_id(1) == 0)
    def _():
        @pl.when(pl.program_id(0) % tiles_per_seq == 0)
        def _():
            halo_ref[...] = jnp.zeros_like(halo_ref)

        hg = p_ref[...].astype(F32) * _sigmoid(q_ref[...].astype(F32))
        halo = halo_ref[...]
        for r in range(V7X_SUBLANES):
            cp_ref[r, pl.ds(r, CONV_HALO), :] = halo
            cp_ref[r, pl.ds(CONV_HALO + r, tm), :] = hg
        halo_ref[...] = hg[tm - CONV_HALO:, :]

        for lc in range(cc // CONV_LANES):
            lanes = slice(lc * CONV_LANES, (lc + 1) * CONV_LANES)
            bias = bdw_ref[:, lanes]

            def body(rc, carry, lanes=lanes, bias=bias):
                base = rc * CONV_ROWS
                acc = jnp.zeros((CONV_ROWS // V7X_SUBLANES, V7X_SUBLANES, CONV_LANES), F32)
                for k in range(kw):
                    delay = kw - 1 - k
                    r = delay % V7X_SUBLANES
                    start = pl.multiple_of(base + (CONV_HALO - (delay - r)), V7X_SUBLANES)
                    win = cp_ref[r, pl.ds(start, CONV_ROWS), lanes]
                    win = win.reshape(CONV_ROWS // V7X_SUBLANES, V7X_SUBLANES, CONV_LANES)
                    acc = acc + win * wdw_ref[k, :, lanes][None]
                conv_ref[pl.ds(pl.multiple_of(base, CONV_ROWS), CONV_ROWS), lanes] = (
                    acc.reshape(CONV_ROWS, CONV_LANES) + bias)
                return carry

            lax.fori_loop(0, tm // CONV_ROWS, body, 0)

        y = _layer_norm(conv_ref[...], lg_ref[...], lb_ref[...])
        act_ref[...] = (y * _sigmoid(y)).astype(BF16)

    yo = jnp.dot(act_ref[...], wo_ref[...], preferred_element_type=F32)
    gate = _sigmoid(gl_ref[...].astype(F32) + bg_ref[...])
    o_ref[...] = (gate * yo).astype(o_ref.dtype)


def _gelu(x):
    return 0.5 * x * (1.0 + lax.erf(x * (1.0 / math.sqrt(2.0))))


def _sgu_kernel(u_ref, v_ref, gl_ref, bg_ref, acc_ref, lg_ref, lb_ref, ws_ref, bs_ref, wo_ref, o_ref,
                s_ref, *, tm, groups, chunk):
    sw = u_ref.shape[1]
    gd = sw // groups
    nchunk = tm // chunk

    @pl.when(pl.program_id(1) == 0)
    def _():
        u = _gelu(u_ref[...].astype(F32))
        v = _gelu(v_ref[...].astype(F32))
        vb = _layer_norm(v, lg_ref[...], lb_ref[...]).astype(BF16)
        row = lax.broadcasted_iota(jnp.int32, (chunk, chunk), 0)
        col = lax.broadcasted_iota(jnp.int32, (chunk, chunk), 1)
        causal = row >= col
        for g in range(groups):
            lanes = slice(g * gd, (g + 1) * gd)
            wg = jnp.where(causal, ws_ref[g], 0.0).astype(BF16)
            vcat = jnp.concatenate([vb[c * chunk:(c + 1) * chunk, lanes] for c in range(nchunk)], axis=1)
            og = jnp.dot(wg, vcat, preferred_element_type=F32)
            for c in range(nchunk):
                rows = slice(c * chunk, (c + 1) * chunk)
                sv = og[:, c * gd:(c + 1) * gd] + bs_ref[:, lanes]
                s_ref[rows, lanes] = (u[rows, lanes] * sv).astype(BF16)

    yo = jnp.dot(s_ref[...], wo_ref[...], preferred_element_type=F32)
    gate = _sigmoid(gl_ref[...].astype(F32) + bg_ref[...])
    o_ref[...] = (acc_ref[...].astype(F32) + gate * yo).astype(o_ref.dtype)


def _xattn_kernel(q_ref, k_ref, v_ref, gl_ref, bg_ref, acc_ref, wo_ref, o_ref, att_ref, *, heads):
    xw = q_ref.shape[1]
    hd = xw // heads

    @pl.when(pl.program_id(1) == 0)
    def _():
        scale = hd ** -0.5
        for h in range(heads):
            lanes = slice(h * hd, (h + 1) * hd)
            s = lax.dot_general(q_ref[:, lanes], k_ref[:, lanes], (((1,), (1,)), ((), ())),
                                preferred_element_type=F32) * scale
            s = s - jnp.max(s, axis=-1, keepdims=True)
            e = jnp.exp(s)
            p = e / jnp.sum(e, axis=-1, keepdims=True)
            oh = jnp.dot(p.astype(BF16), v_ref[:, lanes], preferred_element_type=F32)
            att_ref[:, lanes] = oh.astype(BF16)

    yo = jnp.dot(att_ref[...], wo_ref[...], preferred_element_type=F32)
    gate = _sigmoid(gl_ref[...].astype(F32) + bg_ref[...])
    o_ref[...] = (acc_ref[...].astype(F32) + gate * yo).astype(o_ref.dtype)


def _pack_bf16_pairs(x):
    half = x.shape[1] // 2
    lo = lax.bitcast_convert_type(x[:, :half].astype(BF16).astype(F32), jnp.uint32)
    hi = lax.bitcast_convert_type(x[:, half:].astype(BF16).astype(F32), jnp.uint32)
    return (hi & jnp.uint32(0xFFFF0000)) | (lo >> 16)


def _unpack_bf16_pairs(w):
    lo = lax.bitcast_convert_type(w << 16, F32)
    hi = lax.bitcast_convert_type(w & jnp.uint32(0xFFFF0000), F32)
    return jnp.concatenate([lo, hi], axis=1).astype(BF16)


def _mix_route_kernel(m_ref, x_ref, wo_ref, g_ref, wrh_ref, wrl_ref, br_ref,
                      h_ref, n2_ref, idx_ref, prob_ref, rank_ref, cnt_ref, carry_ref, *, experts):
    tm = x_ref.shape[0]

    @pl.when(pl.program_id(0) == 0)
    def _():
        carry_ref[...] = jnp.zeros_like(carry_ref)

    h = x_ref[...] + jnp.dot(m_ref[...], wo_ref[...], preferred_element_type=F32)
    h_ref[...] = h
    n2 = _rms_norm(h, g_ref[...])
    n2_ref[...] = _pack_bf16_pairs(n2)

    n_hi = n2.astype(BF16)
    n_lo = (n2 - n_hi.astype(F32)).astype(BF16)
    nt = (((1,), (1,)), ((), ()))
    lg = (lax.dot_general(wrh_ref[...], n_hi, nt, preferred_element_type=F32)
          + lax.dot_general(wrh_ref[...], n_lo, nt, preferred_element_type=F32)
          + lax.dot_general(wrl_ref[...], n_hi, nt, preferred_element_type=F32)
          + br_ref[...])

    eidx = lax.broadcasted_iota(jnp.int32, (experts, tm), 0)
    vals, onehots = [], []
    for k in range(TOP_K):
        mx = jnp.max(lg, axis=0, keepdims=True)
        ik = jnp.min(jnp.where(lg == mx, eidx, experts), axis=0, keepdims=True)
        sel = eidx == ik
        vals.append(mx)
        onehots.append(sel)
        idx_ref[k:k + 1, :] = ik
        lg = jnp.where(sel, -jnp.inf, lg)

    exps = [jnp.exp(v - vals[0]) for v in vals]
    denom = exps[0] + exps[1] + exps[2] + exps[3]
    for k in range(TOP_K):
        prob_ref[k:k + 1, :] = exps[k] / denom

    chosen = jnp.zeros((experts, tm), F32)
    for sel in onehots:
        chosen = chosen + sel.astype(F32)
    src = lax.broadcasted_iota(jnp.int32, (tm, tm), 0)
    dst = lax.broadcasted_iota(jnp.int32, (tm, tm), 1)
    before = (src < dst).astype(BF16)
    prior = jnp.dot(chosen.astype(BF16), before, preferred_element_type=F32) + carry_ref[...]
    for k in range(TOP_K):
        rk = jnp.sum(jnp.where(onehots[k], prior, 0.0), axis=0, keepdims=True)
        rank_ref[k:k + 1, :] = rk.astype(jnp.int32)
    carry_ref[...] = carry_ref[...] + jnp.sum(chosen, axis=1, keepdims=True)
    cnt_ref[...] = jnp.broadcast_to(carry_ref[...], cnt_ref.shape).astype(jnp.int32)


def _route_table_kernel(cnt_ref, idx_ref, rank_ref, dest_ref, meta_ref, *, experts, tile_log2):
    lanes = cnt_ref.shape[1]
    cnt = cnt_ref[...]
    padded = ((cnt + ((1 << tile_log2) - 1)) >> tile_log2) << tile_log2
    padded_f = padded.astype(F32)
    row = lax.broadcasted_iota(jnp.int32, (experts, lanes), 0)
    lane = lax.broadcasted_iota(jnp.int32, (experts, lanes), 1)
    padded_row = jnp.sum(jnp.where(row == lane, padded_f, 0.0), axis=0, keepdims=True)
    pstart_col = jnp.sum(jnp.where(lane < row, padded_row, 0.0), axis=1, keepdims=True)
    pend_col = pstart_col + padded_f[:, 0:1]

    tt = idx_ref.shape[1]
    eidx = lax.broadcasted_iota(jnp.int32, (experts, tt), 0)
    for k in range(TOP_K):
        sel = eidx == idx_ref[k:k + 1, :]
        start = jnp.sum(jnp.where(sel, pstart_col, 0.0), axis=0, keepdims=True)
        dest_ref[k:k + 1, :] = rank_ref[k:k + 1, :] + start.astype(jnp.int32)

    nb = meta_ref.shape[1]
    tile_start = (lax.broadcasted_iota(jnp.int32, (experts, nb), 1) << tile_log2).astype(F32)
    ecol = lax.broadcasted_iota(jnp.int32, (experts, nb), 0)
    be = jnp.minimum(jnp.sum((pend_col <= tile_start).astype(F32), axis=0, keepdims=True).astype(jnp.int32),
                     experts - 1)
    nonempty = padded_f[:, 0:1] > 0.0
    nxt = jnp.min(jnp.where(nonempty & (ecol > be), ecol, experts), axis=0, keepdims=True)
    first = jnp.min(jnp.where(nonempty, ecol, experts), axis=0, keepdims=True)
    total = jnp.sum(padded_f, axis=0, keepdims=True)[:, 0:1].astype(jnp.int32) >> tile_log2
    meta_ref[...] = jnp.zeros_like(meta_ref)
    meta_ref[0:1, :] = be
    meta_ref[1:2, :] = nxt
    meta_ref[2:3, :] = jnp.broadcast_to(total, (1, nb))
    meta_ref[3:4, :] = first


def _dispatch_kernel(dest_ref, n2_ref, xp_in_ref, xp_ref, sem):
    del xp_in_ref
    tm = n2_ref.shape[0]

    def row_copy(t, d):
        return pltpu.make_async_copy(n2_ref.at[pl.ds(t, 1)], xp_ref.at[pl.ds(d, 1)], sem)

    def issue(t, carry):
        for k in range(TOP_K):
            row_copy(t, dest_ref[k, t]).start()
        return carry

    lax.fori_loop(0, tm, issue, 0)

    def drain(t, carry):
        for k in range(TOP_K):
            row_copy(t, dest_ref[k, t]).wait()
        return carry

    lax.fori_loop(0, tm, drain, 0)


def _stream_expert_weights(be_ref, nx_ref, fe_ref, copies, load, *, experts, n_pass):
    p = pl.program_id(0)
    m = pl.program_id(1)
    e = be_ref[m]

    @pl.when((m == 0) | (e != be_ref[jnp.maximum(m - 1, 0)]))
    def _():
        @pl.when((m == 0) & (p == 0))
        def _():
            for c in copies(e, p):
                c.start()

        for c in copies(e, p):
            c.wait()
        load()
        same_pass = nx_ref[m] < experts
        e2 = jnp.where(same_pass, nx_ref[m], fe_ref[0])
        p2 = jnp.where(same_pass, p, p + 1)

        @pl.when(p2 < n_pass)
        def _():
            for c in copies(e2, p2):
                c.start()


def _expert_up_kernel(be_ref, nx_ref, nu_ref, fe_ref, x_ref, bg_ref, bu_ref, w_hbm, o_ref,
                      stage_ref, wgb_ref, wub_ref, sem, *, experts, n_pass):
    m = pl.program_id(1)
    tn = o_ref.shape[1]
    de = n_pass * tn

    def copies(e, p):
        return [pltpu.make_async_copy(w_hbm.at[e, :, pl.ds(pl.multiple_of(half * de + p * tn, tn), tn)],
                                      stage_ref.at[half], sem.at[half]) for half in range(2)]

    def load():
        wgb_ref[...] = stage_ref[0].astype(BF16)
        wub_ref[...] = stage_ref[1].astype(BF16)

    @pl.when(m < nu_ref[0])
    def _():
        _stream_expert_weights(be_ref, nx_ref, fe_ref, copies, load, experts=experts, n_pass=n_pass)
        x = _unpack_bf16_pairs(x_ref[...])
        g = jnp.dot(x, wgb_ref[...], preferred_element_type=F32) + bg_ref[0]
        u = jnp.dot(x, wub_ref[...], preferred_element_type=F32) + bu_ref[0]
        g = jnp.minimum(g, SWIGLU_LIMIT)
        u = jnp.clip(u, -SWIGLU_LIMIT, SWIGLU_LIMIT)
        o_ref[...] = ((u + 1.0) * (g * _sigmoid(SWIGLU_ALPHA * g))).astype(o_ref.dtype)

    @pl.when(m >= nu_ref[0])
    def _():
        o_ref[...] = jnp.zeros_like(o_ref)


def _expert_down_kernel(be_ref, nx_ref, nu_ref, fe_ref, h_ref, b_ref, w_hbm, o_ref, stage_ref, wb_ref, sem,
                        *, experts, n_pass):
    m = pl.program_id(1)
    tn = o_ref.shape[1]

    def copies(e, p):
        return [pltpu.make_async_copy(w_hbm.at[e, :, pl.ds(pl.multiple_of(p * tn, tn), tn)], stage_ref, sem)]

    def load():
        wb_ref[...] = stage_ref[...].astype(BF16)

    @pl.when(m < nu_ref[0])
    def _():
        _stream_expert_weights(be_ref, nx_ref, fe_ref, copies, load, experts=experts, n_pass=n_pass)
        o_ref[...] = jnp.dot(h_ref[...], wb_ref[...], preferred_element_type=F32) + b_ref[0]

    @pl.when(m >= nu_ref[0])
    def _():
        o_ref[...] = jnp.zeros_like(o_ref)


def _combine_kernel(dest_ref, p_ref, h_ref, g_ref, yp_ref, o_ref, rows_ref, sem):
    tm = h_ref.shape[0]

    def row_copy(t, k):
        return pltpu.make_async_copy(yp_ref.at[pl.ds(dest_ref[k, t], 1)], rows_ref.at[k, pl.ds(t, 1)], sem)

    def issue(t, carry):
        for k in range(TOP_K):
            row_copy(t, k).start()
        return carry

    lax.fori_loop(0, tm, issue, 0)

    def drain(t, carry):
        for k in range(TOP_K):
            row_copy(t, k).wait()
        return carry

    lax.fori_loop(0, tm, drain, 0)

    y = h_ref[...]
    for k in range(TOP_K):
        y = y + p_ref[:, k:k + 1] * rows_ref[k]
    o_ref[...] = _rms_norm(y, g_ref[...])


def kernel(x, mem, norm1_g, w_in, b_gate, w_dw, b_dw, conv_ln_g, conv_ln_b, w_conv_out, sgu_ln_g, sgu_ln_b,
           w_spatial, b_spatial, w_sgu_out, mem_norm_g, w_kv, w_xattn_out, w_out, norm2_g, w_router, b_router,
           w_gate_up, b_gate_up, w_down, b_down, final_norm_g):
    b, s, d = x.shape
    depth = norm1_g.shape[0]
    t = b * s
    nm = mem.shape[1]
    kw, cc = w_dw.shape[1], w_dw.shape[2]
    sw = w_sgu_out.shape[1]
    groups, chunk = w_spatial.shape[1], w_spatial.shape[2]
    xw = w_xattn_out.shape[1]
    heads = xw // (d // XATTN_HEAD_DIM_DIVISOR)
    experts = w_router.shape[2]
    de = w_down.shape[2]
    in_cols = w_in.shape[2]

    unit = math.gcd(math.gcd(cc, sw), math.gcd(xw, d))
    assert in_cols == 2 * cc + 2 * sw + xw + 3 * d and in_cols % unit == 0
    assert (sw // groups) % V7X_LANES == 0 and kw - 1 <= CONV_HALO and cc % CONV_LANES == 0
    assert (2 * cc) % sw == 0 and (2 * cc + 2 * sw) % xw == 0 and d % 2 == 0

    tm = min(512, s)
    assert s % tm == 0 and tm % chunk == 0 and tm % CONV_ROWS == 0
    tiles_per_seq = s // tm
    nj = d // unit
    gate_col0 = (2 * cc + 2 * sw + xw) // unit

    h = x.reshape(t, d)
    for l in range(depth):
        proj = _norm_matmul(h, norm1_g[l], w_in[l].astype(BF16), min(1024, t), unit)
        kv = _norm_matmul(mem.reshape(b * nm, d), mem_norm_g[l], w_kv[l].astype(BF16), min(512, b * nm), xw)
        bg = b_gate[l].reshape(1, 3 * d)

        def gate_specs(branch):
            return [pl.BlockSpec((tm, unit), lambda i, j: (i, gate_col0 + branch * nj + j)),
                    pl.BlockSpec((1, unit), lambda i, j: (0, branch * nj + j))]

        tile_spec = pl.BlockSpec((tm, unit), lambda i, j: (i, j))
        row_vec = lambda n: pl.BlockSpec((1, n), lambda i, j: (0, 0))

        wdw_b = jnp.broadcast_to(w_dw[l][:, None, :], (kw, V7X_SUBLANES, cc))
        gy = pl.pallas_call(
            functools.partial(_conv_kernel, tm=tm, tiles_per_seq=tiles_per_seq, kw=kw),
            grid=(t // tm, nj),
            in_specs=[pl.BlockSpec((tm, cc), lambda i, j: (i, 0)),
                      pl.BlockSpec((tm, cc), lambda i, j: (i, 1)),
                      *gate_specs(0),
                      pl.BlockSpec((kw, V7X_SUBLANES, cc), lambda i, j: (0, 0, 0)),
                      row_vec(cc), row_vec(cc), row_vec(cc),
                      pl.BlockSpec((cc, unit), lambda i, j: (0, j))],
            out_specs=tile_spec,
            out_shape=jax.ShapeDtypeStruct((t, d), BF16),
            scratch_shapes=[pltpu.VMEM((V7X_SUBLANES, tm + CONV_HALO + V7X_SUBLANES, cc), F32),
                            pltpu.VMEM((CONV_HALO, cc), F32),
                            pltpu.VMEM((tm, cc), F32),
                            pltpu.VMEM((tm, cc), BF16)],
            compiler_params=_params("arbitrary", "arbitrary"),
            name="conv_branch",
        )(proj, proj, proj, bg, wdw_b, b_dw[l].reshape(1, cc), conv_ln_g[l].reshape(1, cc),
          conv_ln_b[l].reshape(1, cc), w_conv_out[l].astype(BF16))

        bs_full = jnp.repeat(b_spatial[l].T, sw // groups, axis=1)
        gy = pl.pallas_call(
            functools.partial(_sgu_kernel, tm=tm, groups=groups, chunk=chunk),
            grid=(t // tm, nj),
            in_specs=[pl.BlockSpec((tm, sw), lambda i, j: (i, 2 * cc // sw)),
                      pl.BlockSpec((tm, sw), lambda i, j: (i, 2 * cc // sw + 1)),
                      *gate_specs(1),
                      tile_spec,
                      row_vec(sw), row_vec(sw),
                      pl.BlockSpec((groups, chunk, chunk), lambda i, j: (0, 0, 0)),
                      pl.BlockSpec((chunk, sw), lambda i, j: (0, 0)),
                      pl.BlockSpec((sw, unit), lambda i, j: (0, j))],
            out_specs=tile_spec,
            out_shape=jax.ShapeDtypeStruct((t, d), BF16),
            scratch_shapes=[pltpu.VMEM((tm, sw), BF16)],
            compiler_params=_params("arbitrary", "arbitrary"),
            name="sgu_branch",
        )(proj, proj, proj, bg, gy, sgu_ln_g[l].reshape(1, sw), sgu_ln_b[l].reshape(1, sw),
          w_spatial[l], bs_full, w_sgu_out[l].astype(BF16))

        merged = pl.pallas_call(
            functools.partial(_xattn_kernel, heads=heads),
            grid=(t // tm, nj),
            in_specs=[pl.BlockSpec((tm, xw), lambda i, j: (i, (2 * cc + 2 * sw) // xw)),
                      pl.BlockSpec((nm, xw), lambda i, j: (i // tiles_per_seq, 0)),
                      pl.BlockSpec((nm, xw), lambda i, j: (i // tiles_per_seq, 1)),
                      *gate_specs(2),
                      tile_spec,
                      pl.BlockSpec((xw, unit), lambda i, j: (0, j))],
            out_specs=tile_spec,
            out_shape=jax.ShapeDtypeStruct((t, d), BF16),
            scratch_shapes=[pltpu.VMEM((tm, xw), BF16)],
            compiler_params=_params("arbitrary", "arbitrary"),
            name="xattn_branch",
        )(proj, kv, kv, proj, bg, gy, w_xattn_out[l].astype(BF16))

        wr = w_router[l].T
        wr_hi = wr.astype(BF16)
        wr_lo = (wr - wr_hi.astype(F32)).astype(BF16)
        tr = min(512, t)
        full = lambda shape: pl.BlockSpec(shape, lambda i: tuple(0 for _ in shape))
        kt_spec = pl.BlockSpec((TOP_K, tr), lambda i: (0, i))
        h, n2p, top_idx, probs, rank, counts = pl.pallas_call(
            functools.partial(_mix_route_kernel, experts=experts),
            grid=(t // tr,),
            in_specs=[pl.BlockSpec((tr, d), lambda i: (i, 0)),
                      pl.BlockSpec((tr, d), lambda i: (i, 0)),
                      full((d, d)), full((1, d)), full((experts, d)), full((experts, d)), full((experts, 1))],
            out_specs=[pl.BlockSpec((tr, d), lambda i: (i, 0)),
                       pl.BlockSpec((tr, d // 2), lambda i: (i, 0)),
                       kt_spec, kt_spec, kt_spec,
                       full((experts, V7X_LANES))],
            out_shape=[jax.ShapeDtypeStruct((t, d), F32),
                       jax.ShapeDtypeStruct((t, d // 2), jnp.uint32),
                       jax.ShapeDtypeStruct((TOP_K, t), jnp.int32),
                       jax.ShapeDtypeStruct((TOP_K, t), F32),
                       jax.ShapeDtypeStruct((TOP_K, t), jnp.int32),
                       jax.ShapeDtypeStruct((experts, V7X_LANES), jnp.int32)],
            scratch_shapes=[pltpu.VMEM((experts, 1), F32)],
            compiler_params=_params("arbitrary"),
            name="mix_route",
        )(merged, h, w_out[l].astype(BF16), norm2_g[l].reshape(1, d), wr_hi, wr_lo, b_router[l].reshape(experts, 1))

        tile_log2 = 8
        te = 1 << tile_log2
        n_blocks = (t * TOP_K) // te + experts
        cap = n_blocks * te
        nb_pad = -(-n_blocks // V7X_LANES) * V7X_LANES
        tt = min(2048, t)
        dest, meta = pl.pallas_call(
            functools.partial(_route_table_kernel, experts=experts, tile_log2=tile_log2),
            grid=(t // tt,),
            in_specs=[pl.BlockSpec((experts, V7X_LANES), lambda i: (0, 0)),
                      pl.BlockSpec((TOP_K, tt), lambda i: (0, i)),
                      pl.BlockSpec((TOP_K, tt), lambda i: (0, i))],
            out_specs=[pl.BlockSpec((TOP_K, tt), lambda i: (0, i)),
                       pl.BlockSpec((V7X_SUBLANES, nb_pad), lambda i: (0, 0))],
            out_shape=[jax.ShapeDtypeStruct((TOP_K, t), jnp.int32),
                       jax.ShapeDtypeStruct((V7X_SUBLANES, nb_pad), jnp.int32)],
            compiler_params=_params("arbitrary"),
            name="route_table",
        )(counts, top_idx, rank)
        block_e, next_e = meta[0, :n_blocks], meta[1, :n_blocks]
        n_used, first_e = meta[2, :1], meta[3, :1]

        td = min(256, t)
        xp = pl.pallas_call(
            _dispatch_kernel,
            grid=(t // td,),
            in_specs=[pl.BlockSpec((TOP_K, td), lambda i: (0, i), memory_space=pltpu.SMEM),
                      pl.BlockSpec((td, d // 2), lambda i: (i, 0)),
                      pl.BlockSpec(memory_space=pl.ANY)],
            out_specs=pl.BlockSpec(memory_space=pl.ANY),
            out_shape=jax.ShapeDtypeStruct((cap, d // 2), jnp.uint32),
            scratch_shapes=[pltpu.SemaphoreType.DMA(())],
            input_output_aliases={2: 0},
            compiler_params=_params("arbitrary"),
            name="dispatch",
        )(dest, n2p, jnp.zeros((cap, d // 2), jnp.uint32))

        tn_up = min(1024, de)
        n_up = de // tn_up
        clamp = lambda m, nu: jnp.minimum(m, nu[0] - 1)
        bias_up = b_gate_up[l].reshape(experts, 1, 2 * de)
        hidden = pl.pallas_call(
            functools.partial(_expert_up_kernel, experts=experts, n_pass=n_up),
            grid_spec=pltpu.PrefetchScalarGridSpec(
                num_scalar_prefetch=4,
                grid=(n_up, n_blocks),
                in_specs=[pl.BlockSpec((te, d // 2), lambda p, m, be, nx, nu, fe: (clamp(m, nu), 0)),
                          pl.BlockSpec((1, 1, tn_up), lambda p, m, be, nx, nu, fe: (be[clamp(m, nu)], 0, p)),
                          pl.BlockSpec((1, 1, tn_up),
                                       lambda p, m, be, nx, nu, fe: (be[clamp(m, nu)], 0, n_up + p)),
                          pl.BlockSpec(memory_space=pl.ANY)],
                out_specs=pl.BlockSpec((te, tn_up), lambda p, m, be, nx, nu, fe: (m, p)),
                scratch_shapes=[pltpu.VMEM((2, d, tn_up), F32),
                                pltpu.VMEM((d, tn_up), BF16), pltpu.VMEM((d, tn_up), BF16),
                                pltpu.SemaphoreType.DMA((2,))]),
            out_shape=jax.ShapeDtypeStruct((cap, de), BF16),
            compiler_params=_params("arbitrary", "arbitrary"),
            name="expert_up",
        )(block_e, next_e, n_used, first_e, xp, bias_up, bias_up, w_gate_up[l])

        tn_dn = min(2048, d)
        n_dn = d // tn_dn
        yp = pl.pallas_call(
            functools.partial(_expert_down_kernel, experts=experts, n_pass=n_dn),
            grid_spec=pltpu.PrefetchScalarGridSpec(
                num_scalar_prefetch=4,
                grid=(n_dn, n_blocks),
                in_specs=[pl.BlockSpec((te, de), lambda p, m, be, nx, nu, fe: (clamp(m, nu), 0)),
                          pl.BlockSpec((1, 1, tn_dn), lambda p, m, be, nx, nu, fe: (be[clamp(m, nu)], 0, p)),
                          pl.BlockSpec(memory_space=pl.ANY)],
                out_specs=pl.BlockSpec((te, tn_dn), lambda p, m, be, nx, nu, fe: (m, p)),
                scratch_shapes=[pltpu.VMEM((de, tn_dn), F32), pltpu.VMEM((de, tn_dn), BF16),
                                pltpu.SemaphoreType.DMA(())]),
            out_shape=jax.ShapeDtypeStruct((cap, d), F32),
            compiler_params=_params("arbitrary", "arbitrary"),
            name="expert_down",
        )(block_e, next_e, n_used, first_e, hidden, b_down[l].reshape(experts, 1, d), w_down[l])

        assert depth == 1
        tc = min(256, t)
        h = pl.pallas_call(
            _combine_kernel,
            grid=(t // tc,),
            in_specs=[pl.BlockSpec((TOP_K, tc), lambda i: (0, i), memory_space=pltpu.SMEM),
                      pl.BlockSpec((tc, TOP_K), lambda i: (i, 0)),
                      pl.BlockSpec((tc, d), lambda i: (i, 0)),
                      pl.BlockSpec((1, d), lambda i: (0, 0)),
                      pl.BlockSpec(memory_space=pl.ANY)],
            out_specs=pl.BlockSpec((tc, d), lambda i: (i, 0)),
            out_shape=jax.ShapeDtypeStruct((t, d), F32),
            scratch_shapes=[pltpu.VMEM((TOP_K, tc, d), F32), pltpu.SemaphoreType.DMA(())],
            compiler_params=_params("arbitrary"),
            name="combine",
        )(dest, probs.T, h, final_norm_g.reshape(1, d), yp)

    return h.reshape(b, s, d)
```

```python
import functools
import math

import jax
import jax.numpy as jnp
from jax import lax
from jax.experimental import pallas as pl
from jax.experimental.pallas import tpu as pltpu

EPS = 1e-5
TOP_K = 4
SWIGLU_LIMIT = 7.0
SWIGLU_ALPHA = 1.702
XATTN_HEAD_DIM_DIVISOR = 8

V7X_LANES = 128
V7X_SUBLANES = 8
V7X_VMEM_LIMIT_BYTES = 56 * 1024 * 1024

F32 = jnp.float32
BF16 = jnp.bfloat16


def _params(*sem):
    return pltpu.CompilerParams(dimension_semantics=sem, vmem_limit_bytes=V7X_VMEM_LIMIT_BYTES)


def _sigmoid(x):
    return 1.0 / (1.0 + jnp.exp(-x))


def _layer_norm(x, g, b):
    mu = jnp.mean(x, axis=-1, keepdims=True)
    xc = x - mu
    var = jnp.mean(xc * xc, axis=-1, keepdims=True)
    return xc * lax.rsqrt(var + EPS) * g + b


def _rms_norm(x, g):
    return x * lax.rsqrt(jnp.mean(x * x, axis=-1, keepdims=True) + EPS) * g


def _norm_matmul_kernel(x_ref, g_ref, w_ref, o_ref, xn_ref):
    @pl.when(pl.program_id(1) == 0)
    def _():
        xn_ref[...] = _rms_norm(x_ref[...], g_ref[...]).astype(BF16)

    o_ref[...] = jnp.dot(xn_ref[...], w_ref[...], preferred_element_type=F32).astype(o_ref.dtype)


def _norm_matmul(x2d, g, w_bf16, tm, tn):
    m, d = x2d.shape
    n = w_bf16.shape[1]
    return pl.pallas_call(
        _norm_matmul_kernel,
        grid=(m // tm, n // tn),
        in_specs=[
            pl.BlockSpec((tm, d), lambda i, j: (i, 0)),
            pl.BlockSpec((1, d), lambda i, j: (0, 0)),
            pl.BlockSpec((d, tn), lambda i, j: (0, j)),
        ],
        out_specs=pl.BlockSpec((tm, tn), lambda i, j: (i, j)),
        out_shape=jax.ShapeDtypeStruct((m, n), BF16),
        scratch_shapes=[pltpu.VMEM((tm, d), BF16)],
        compiler_params=_params("arbitrary", "arbitrary"),
        name="norm_matmul",
    )(x2d, g.reshape(1, d), w_bf16)


CONV_HALO = 32
CONV_ROWS = 64
CONV_LANES = 256


def _conv_kernel(p_ref, q_ref, gl_ref, bg_ref, wdw_ref, bdw_ref, lg_ref, lb_ref, wo_ref, o_ref,
                 cp_ref, halo_ref, conv_ref, act_ref, *, tm, tiles_per_seq, kw):
    cc = p_ref.shape[1]
    back = (kw - 1) // V7X_SUBLANES * V7X_SUBLANES

    @pl.when(pl.program_id(1) == 0)
    def _():
        @pl.when(pl.program_id(0) % tiles_per_seq == 0)
        def _():
            halo_ref[...] = jnp.zeros_like(halo_ref)

        hg = p_ref[...].astype(F32) * _sigmoid(q_ref[...].astype(F32))
        halo = halo_ref[...]
        for r in range(V7X_SUBLANES):
            cp_ref[r, pl.ds(r, CONV_HALO), :] = halo
            cp_ref[r, pl.ds(CONV_HALO + r, tm), :] = hg
        halo_ref[...] = hg[tm - CONV_HALO:, :]

        for lc in range(cc // CONV_LANES):
            lanes = slice(lc * CONV_LANES, (lc + 1) * CONV_LANES)
            bias = bdw_ref[:, lanes]

            def body(rc, carry, lanes=lanes, bias=bias):
                base = rc * CONV_ROWS
                acc = jnp.zeros((CONV_ROWS // V7X_SUBLANES, V7X_SUBLANES, CONV_LANES), F32)
                for r in range(V7X_SUBLANES):
                    start = pl.multiple_of(base + (CONV_HALO - back), V7X_SUBLANES)
                    big = cp_ref[r, pl.ds(start, CONV_ROWS + back), lanes]
                    for k in range(kw):
                        delay = kw - 1 - k
                        if delay % V7X_SUBLANES != r:
                            continue
                        off = back - (delay - r)
                        win = big[off:off + CONV_ROWS].reshape(CONV_ROWS // V7X_SUBLANES, V7X_SUBLANES, CONV_LANES)
                        acc = acc + win * wdw_ref[k, :, lanes][None]
                conv_ref[pl.ds(pl.multiple_of(base, CONV_ROWS), CONV_ROWS), lanes] = (
                    acc.reshape(CONV_ROWS, CONV_LANES) + bias)
                return carry

            lax.fori_loop(0, tm // CONV_ROWS, body, 0)

        y = _layer_norm(conv_ref[...], lg_ref[...], lb_ref[...])
        act_ref[...] = (y * _sigmoid(y)).astype(BF16)

    yo = jnp.dot(act_ref[...], wo_ref[...], preferred_element_type=F32)
    gate = _sigmoid(gl_ref[...].astype(F32) + bg_ref[...])
    o_ref[...] = (gate * yo).astype(o_ref.dtype)


def _gelu(x):
    return 0.5 * x * (1.0 + lax.erf(x * (1.0 / math.sqrt(2.0))))


def _sgu_kernel(u_ref, v_ref, gl_ref, bg_ref, acc_ref, lg_ref, lb_ref, ws_ref, bs_ref, wo_ref, o_ref,
                s_ref, *, tm, groups, chunk):
    sw = u_ref.shape[1]
    gd = sw // groups
    nchunk = tm // chunk

    @pl.when(pl.program_id(1) == 0)
    def _():
        u = _gelu(u_ref[...].astype(F32))
        v = _gelu(v_ref[...].astype(F32))
        vb = _layer_norm(v, lg_ref[...], lb_ref[...]).astype(BF16)
        row = lax.broadcasted_iota(jnp.int32, (chunk, chunk), 0)
        col = lax.broadcasted_iota(jnp.int32, (chunk, chunk), 1)
        causal = row >= col
        for g in range(groups):
            lanes = slice(g * gd, (g + 1) * gd)
            wg = jnp.where(causal, ws_ref[g], 0.0).astype(BF16)
            vcat = jnp.concatenate([vb[c * chunk:(c + 1) * chunk, lanes] for c in range(nchunk)], axis=1)
            og = jnp.dot(wg, vcat, preferred_element_type=F32)
            for c in range(nchunk):
                rows = slice(c * chunk, (c + 1) * chunk)
                sv = og[:, c * gd:(c + 1) * gd] + bs_ref[:, lanes]
                s_ref[rows, lanes] = (u[rows, lanes] * sv).astype(BF16)

    yo = jnp.dot(s_ref[...], wo_ref[...], preferred_element_type=F32)
    gate = _sigmoid(gl_ref[...].astype(F32) + bg_ref[...])
    o_ref[...] = (acc_ref[...].astype(F32) + gate * yo).astype(o_ref.dtype)


def _xattn_kernel(q_ref, k_ref, v_ref, gl_ref, bg_ref, acc_ref, wo_ref, o_ref, att_ref, *, heads):
    xw = q_ref.shape[1]
    hd = xw // heads

    @pl.when(pl.program_id(1) == 0)
    def _():
        scale = hd ** -0.5
        for h in range(heads):
            lanes = slice(h * hd, (h + 1) * hd)
            s = lax.dot_general(q_ref[:, lanes], k_ref[:, lanes], (((1,), (1,)), ((), ())),
                                preferred_element_type=F32) * scale
            s = s - jnp.max(s, axis=-1, keepdims=True)
            e = jnp.exp(s)
            p = e / jnp.sum(e, axis=-1, keepdims=True)
            oh = jnp.dot(p.astype(BF16), v_ref[:, lanes], preferred_element_type=F32)
            att_ref[:, lanes] = oh.astype(BF16)

    yo = jnp.dot(att_ref[...], wo_ref[...], preferred_element_type=F32)
    gate = _sigmoid(gl_ref[...].astype(F32) + bg_ref[...])
    o_ref[...] = (acc_ref[...].astype(F32) + gate * yo).astype(o_ref.dtype)


def _mix_route_kernel(m_ref, x_ref, wo_ref, g_ref, wrh_ref, wrl_ref, br_ref,
                      h_ref, n2_ref, idx_ref, prob_ref, rank_ref, cnt_ref, carry_ref, *, experts):
    tm = x_ref.shape[0]

    @pl.when(pl.program_id(0) == 0)
    def _():
        carry_ref[...] = jnp.zeros_like(carry_ref)

    h = x_ref[...] + jnp.dot(m_ref[...], wo_ref[...], preferred_element_type=F32)
    h_ref[...] = h
    n2 = _rms_norm(h, g_ref[...])
    n2_ref[...] = n2

    n_hi = n2.astype(BF16)
    n_lo = (n2 - n_hi.astype(F32)).astype(BF16)
    nt = (((1,), (1,)), ((), ()))
    lg = (lax.dot_general(wrh_ref[...], n_hi, nt, preferred_element_type=F32)
          + lax.dot_general(wrh_ref[...], n_lo, nt, preferred_element_type=F32)
          + lax.dot_general(wrl_ref[...], n_hi, nt, preferred_element_type=F32)
          + br_ref[...])

    eidx = lax.broadcasted_iota(jnp.int32, (experts, tm), 0)
    vals, onehots = [], []
    for k in range(TOP_K):
        mx = jnp.max(lg, axis=0, keepdims=True)
        ik = jnp.min(jnp.where(lg == mx, eidx, experts), axis=0, keepdims=True)
        sel = eidx == ik
        vals.append(mx)
        onehots.append(sel)
        idx_ref[k:k + 1, :] = ik
        lg = jnp.where(sel, -jnp.inf, lg)

    exps = [jnp.exp(v - vals[0]) for v in vals]
    denom = exps[0] + exps[1] + exps[2] + exps[3]
    for k in range(TOP_K):
        prob_ref[k:k + 1, :] = exps[k] / denom

    chosen = jnp.zeros((experts, tm), F32)
    for sel in onehots:
        chosen = chosen + sel.astype(F32)
    src = lax.broadcasted_iota(jnp.int32, (tm, tm), 0)
    dst = lax.broadcasted_iota(jnp.int32, (tm, tm), 1)
    before = (src < dst).astype(BF16)
    prior = jnp.dot(chosen.astype(BF16), before, preferred_element_type=F32) + carry_ref[...]
    for k in range(TOP_K):
        rk = jnp.sum(jnp.where(onehots[k], prior, 0.0), axis=0, keepdims=True)
        rank_ref[k:k + 1, :] = rk.astype(jnp.int32)
    carry_ref[...] = carry_ref[...] + jnp.sum(chosen, axis=1, keepdims=True)
    cnt_ref[...] = jnp.broadcast_to(carry_ref[...], cnt_ref.shape).astype(jnp.int32)


def _route_table_kernel(cnt_ref, idx_ref, rank_ref, dest_ref, meta_ref, *, experts, tile_log2):
    lanes = cnt_ref.shape[1]
    cnt = cnt_ref[...]
    padded = ((cnt + ((1 << tile_log2) - 1)) >> tile_log2) << tile_log2
    padded_f = padded.astype(F32)
    row = lax.broadcasted_iota(jnp.int32, (experts, lanes), 0)
    lane = lax.broadcasted_iota(jnp.int32, (experts, lanes), 1)
    padded_row = jnp.sum(jnp.where(row == lane, padded_f, 0.0), axis=0, keepdims=True)
    pstart_col = jnp.sum(jnp.where(lane < row, padded_row, 0.0), axis=1, keepdims=True)
    pend_col = pstart_col + padded_f[:, 0:1]

    tt = idx_ref.shape[1]
    eidx = lax.broadcasted_iota(jnp.int32, (experts, tt), 0)
    for k in range(TOP_K):
        sel = eidx == idx_ref[k:k + 1, :]
        start = jnp.sum(jnp.where(sel, pstart_col, 0.0), axis=0, keepdims=True)
        dest_ref[k:k + 1, :] = rank_ref[k:k + 1, :] + start.astype(jnp.int32)

    nb = meta_ref.shape[1]
    tile_start = (lax.broadcasted_iota(jnp.int32, (experts, nb), 1) << tile_log2).astype(F32)
    ecol = lax.broadcasted_iota(jnp.int32, (experts, nb), 0)
    be = jnp.minimum(jnp.sum((pend_col <= tile_start).astype(F32), axis=0, keepdims=True).astype(jnp.int32),
                     experts - 1)
    nonempty = padded_f[:, 0:1] > 0.0
    nxt = jnp.min(jnp.where(nonempty & (ecol > be), ecol, experts), axis=0, keepdims=True)
    first = jnp.min(jnp.where(nonempty, ecol, experts), axis=0, keepdims=True)
    total = jnp.sum(padded_f, axis=0, keepdims=True)[:, 0:1].astype(jnp.int32) >> tile_log2
    meta_ref[...] = jnp.zeros_like(meta_ref)
    meta_ref[0:1, :] = be
    meta_ref[1:2, :] = nxt
    meta_ref[2:3, :] = jnp.broadcast_to(total, (1, nb))
    meta_ref[3:4, :] = first
    pstart_row = jnp.sum(jnp.where(row < lane, padded_f, 0.0), axis=0, keepdims=True)
    cnt_row = jnp.sum(jnp.where(row == lane, cnt.astype(F32), 0.0), axis=0, keepdims=True)
    meta_ref[4:5, 0:lanes] = pstart_row.astype(jnp.int32)
    meta_ref[5:6, 0:lanes] = cnt_row.astype(jnp.int32)


def _dispatch_kernel(ps_ref, cnt_ref, nu_ref, dest_ref, n2_ref, xp_ref, zero_ref, sem, zsem,
                     *, experts, n_blocks, tile_log2):
    tm = n2_ref.shape[0]
    te = 1 << tile_log2

    @pl.when(pl.program_id(0) == 0)
    def _():
        zero_ref[...] = jnp.zeros_like(zero_ref)

        def pad_rows(e):
            first = ps_ref[e] + cnt_ref[e]
            return first, ((cnt_ref[e] + (te - 1)) >> tile_log2 << tile_log2) - cnt_ref[e]

        def pad_copy(row):
            return pltpu.make_async_copy(zero_ref.at[pl.ds(0, 1)], xp_ref.at[pl.ds(row, 1)], zsem)

        def tile_copy(b):
            return pltpu.make_async_copy(zero_ref, xp_ref.at[pl.ds(pl.multiple_of(b * te, te), te)], zsem)

        def for_each_fill(fn_row, fn_tile):
            def per_expert(e, carry):
                first, n = pad_rows(e)
                lax.fori_loop(0, n, lambda r, c: (fn_row(first + r), c)[1], 0)
                return carry

            lax.fori_loop(0, experts, per_expert, 0)
            lax.fori_loop(nu_ref[0], n_blocks, lambda b, c: (fn_tile(b), c)[1], 0)

        for_each_fill(lambda row: pad_copy(row).start(), lambda b: tile_copy(b).start())
        for_each_fill(lambda row: pad_copy(row).wait(), lambda b: tile_copy(b).wait())

    def row_copy(t, d):
        return pltpu.make_async_copy(n2_ref.at[pl.ds(t, 1)], xp_ref.at[pl.ds(d, 1)], sem)

    def issue(t, carry):
        for k in range(TOP_K):
            row_copy(t, dest_ref[k, t]).start(priority=k % 2)
        return carry

    lax.fori_loop(0, tm, issue, 0)

    def drain(t, carry):
        for k in range(TOP_K):
            row_copy(t, dest_ref[k, t]).wait()
        return carry

    lax.fori_loop(0, tm, drain, 0)


def _stream_expert_weights(be_ref, nx_ref, fe_ref, copies, load, *, experts, n_pass):
    p = pl.program_id(0)
    m = pl.program_id(1)
    e = be_ref[m]

    @pl.when((m == 0) | (e != be_ref[jnp.maximum(m - 1, 0)]))
    def _():
        @pl.when((m == 0) & (p == 0))
        def _():
            for c in copies(e, p):
                c.start()

        for c in copies(e, p):
            c.wait()
        load()
        same_pass = nx_ref[m] < experts
        e2 = jnp.where(same_pass, nx_ref[m], fe_ref[0])
        p2 = jnp.where(same_pass, p, p + 1)

        @pl.when(p2 < n_pass)
        def _():
            for c in copies(e2, p2):
                c.start(priority=1)


def _expert_up_kernel(be_ref, nx_ref, nu_ref, fe_ref, x_ref, bg_ref, bu_ref, w_hbm, o_ref,
                      stage_ref, wgb_ref, wub_ref, sem, *, experts, n_pass):
    m = pl.program_id(1)
    tn = o_ref.shape[1]
    de = n_pass * tn

    def copies(e, p):
        return [pltpu.make_async_copy(w_hbm.at[e, :, pl.ds(pl.multiple_of(half * de + p * tn, tn), tn)],
                                      stage_ref.at[half], sem.at[half]) for half in range(2)]

    def load():
        wgb_ref[...] = stage_ref[0].astype(BF16)
        wub_ref[...] = stage_ref[1].astype(BF16)

    @pl.when(m < nu_ref[0])
    def _():
        _stream_expert_weights(be_ref, nx_ref, fe_ref, copies, load, experts=experts, n_pass=n_pass)
        x = x_ref[...].astype(BF16)
        g = jnp.dot(x, wgb_ref[...], preferred_element_type=F32) + bg_ref[0]
        u = jnp.dot(x, wub_ref[...], preferred_element_type=F32) + bu_ref[0]
        g = jnp.minimum(g, SWIGLU_LIMIT)
        u = jnp.clip(u, -SWIGLU_LIMIT, SWIGLU_LIMIT)
        o_ref[...] = ((u + 1.0) * (g * _sigmoid(SWIGLU_ALPHA * g))).astype(o_ref.dtype)

    @pl.when(m >= nu_ref[0])
    def _():
        o_ref[...] = jnp.zeros_like(o_ref)


def _expert_down_kernel(be_ref, nx_ref, nu_ref, fe_ref, h_ref, b_ref, w_hbm, o_ref, stage_ref, wb_ref, sem,
                        *, experts, n_pass):
    m = pl.program_id(1)
    tn = o_ref.shape[1]

    def copies(e, p):
        return [pltpu.make_async_copy(w_hbm.at[e, :, pl.ds(pl.multiple_of(p * tn, tn), tn)], stage_ref, sem)]

    def load():
        wb_ref[...] = stage_ref[...].astype(BF16)

    @pl.when(m < nu_ref[0])
    def _():
        _stream_expert_weights(be_ref, nx_ref, fe_ref, copies, load, experts=experts, n_pass=n_pass)
        o_ref[...] = jnp.dot(h_ref[...], wb_ref[...], preferred_element_type=F32) + b_ref[0]

    @pl.when(m >= nu_ref[0])
    def _():
        o_ref[...] = jnp.zeros_like(o_ref)


def _combine_kernel(dest_ref, dnext_ref, p_ref, h_ref, g_ref, yp_ref, o_ref, rows_ref, sem, *, n_tiles):
    i = pl.program_id(0)
    tm = h_ref.shape[0]
    slot = i % 2

    def row_copy(idx_ref, buf, t, k):
        return pltpu.make_async_copy(yp_ref.at[pl.ds(idx_ref[k, t], 1)], rows_ref.at[buf, k, pl.ds(t, 1)],
                                     sem.at[buf])

    def for_each_row(idx_ref, buf, fn):
        def body(t, carry):
            for k in range(TOP_K):
                fn(row_copy(idx_ref, buf, t, k), k)
            return carry

        lax.fori_loop(0, tm, body, 0)

    start = lambda c, k: c.start(priority=k % 2)

    @pl.when(i == 0)
    def _():
        for_each_row(dest_ref, 0, start)

    @pl.when(i + 1 < n_tiles)
    def _():
        for_each_row(dnext_ref, 1 - slot, start)

    for_each_row(dest_ref, slot, lambda c, k: c.wait())

    y = h_ref[...]
    for k in range(TOP_K):
        y = y + p_ref[:, k:k + 1] * rows_ref[slot, k]
    o_ref[...] = _rms_norm(y, g_ref[...])


def kernel(x, mem, norm1_g, w_in, b_gate, w_dw, b_dw, conv_ln_g, conv_ln_b, w_conv_out, sgu_ln_g, sgu_ln_b,
           w_spatial, b_spatial, w_sgu_out, mem_norm_g, w_kv, w_xattn_out, w_out, norm2_g, w_router, b_router,
           w_gate_up, b_gate_up, w_down, b_down, final_norm_g):
    b, s, d = x.shape
    depth = norm1_g.shape[0]
    t = b * s
    nm = mem.shape[1]
    kw, cc = w_dw.shape[1], w_dw.shape[2]
    sw = w_sgu_out.shape[1]
    groups, chunk = w_spatial.shape[1], w_spatial.shape[2]
    xw = w_xattn_out.shape[1]
    heads = xw // (d // XATTN_HEAD_DIM_DIVISOR)
    experts = w_router.shape[2]
    de = w_down.shape[2]
    in_cols = w_in.shape[2]

    unit = math.gcd(math.gcd(cc, sw), math.gcd(xw, d))
    assert in_cols == 2 * cc + 2 * sw + xw + 3 * d and in_cols % unit == 0
    assert (sw // groups) % V7X_LANES == 0 and kw - 1 <= CONV_HALO and cc % CONV_LANES == 0
    assert (2 * cc) % sw == 0 and (2 * cc + 2 * sw) % xw == 0 and d % 2 == 0

    tm = min(512, s)
    assert s % tm == 0 and tm % chunk == 0 and tm % CONV_ROWS == 0
    tiles_per_seq = s // tm
    nj = d // unit
    gate_col0 = (2 * cc + 2 * sw + xw) // unit

    h = x.reshape(t, d)
    for l in range(depth):
        proj = _norm_matmul(h, norm1_g[l], w_in[l].astype(BF16), min(1024, t), unit)
        kv = _norm_matmul(mem.reshape(b * nm, d), mem_norm_g[l], w_kv[l].astype(BF16), min(512, b * nm), xw)
        bg = b_gate[l].reshape(1, 3 * d)

        def gate_specs(branch):
            return [pl.BlockSpec((tm, unit), lambda i, j: (i, gate_col0 + branch * nj + j)),
                    pl.BlockSpec((1, unit), lambda i, j: (0, branch * nj + j))]

        tile_spec = pl.BlockSpec((tm, unit), lambda i, j: (i, j))
        row_vec = lambda n: pl.BlockSpec((1, n), lambda i, j: (0, 0))

        wdw_b = jnp.broadcast_to(w_dw[l][:, None, :], (kw, V7X_SUBLANES, cc))
        gy = pl.pallas_call(
            functools.partial(_conv_kernel, tm=tm, tiles_per_seq=tiles_per_seq, kw=kw),
            grid=(t // tm, nj),
            in_specs=[pl.BlockSpec((tm, cc), lambda i, j: (i, 0)),
                      pl.BlockSpec((tm, cc), lambda i, j: (i, 1)),
                      *gate_specs(0),
                      pl.BlockSpec((kw, V7X_SUBLANES, cc), lambda i, j: (0, 0, 0)),
                      row_vec(cc), row_vec(cc), row_vec(cc),
                      pl.BlockSpec((cc, unit), lambda i, j: (0, j))],
            out_specs=tile_spec,
            out_shape=jax.ShapeDtypeStruct((t, d), BF16),
            scratch_shapes=[pltpu.VMEM((V7X_SUBLANES, tm + CONV_HALO + V7X_SUBLANES, cc), F32),
                            pltpu.VMEM((CONV_HALO, cc), F32),
                            pltpu.VMEM((tm, cc), F32),
                            pltpu.VMEM((tm, cc), BF16)],
            compiler_params=_params("arbitrary", "arbitrary"),
            name="conv_branch",
        )(proj, proj, proj, bg, wdw_b, b_dw[l].reshape(1, cc), conv_ln_g[l].reshape(1, cc),
          conv_ln_b[l].reshape(1, cc), w_conv_out[l].astype(BF16))

        bs_full = jnp.repeat(b_spatial[l].T, sw // groups, axis=1)
        gy = pl.pallas_call(
            functools.partial(_sgu_kernel, tm=tm, groups=groups, chunk=chunk),
            grid=(t // tm, nj),
            in_specs=[pl.BlockSpec((tm, sw), lambda i, j: (i, 2 * cc // sw)),
                      pl.BlockSpec((tm, sw), lambda i, j: (i, 2 * cc // sw + 1)),
                      *gate_specs(1),
                      tile_spec,
                      row_vec(sw), row_vec(sw),
                      pl.BlockSpec((groups, chunk, chunk), lambda i, j: (0, 0, 0)),
                      pl.BlockSpec((chunk, sw), lambda i, j: (0, 0)),
                      pl.BlockSpec((sw, unit), lambda i, j: (0, j))],
            out_specs=tile_spec,
            out_shape=jax.ShapeDtypeStruct((t, d), BF16),
            scratch_shapes=[pltpu.VMEM((tm, sw), BF16)],
            compiler_params=_params("arbitrary", "arbitrary"),
            name="sgu_branch",
        )(proj, proj, proj, bg, gy, sgu_ln_g[l].reshape(1, sw), sgu_ln_b[l].reshape(1, sw),
          w_spatial[l], bs_full, w_sgu_out[l].astype(BF16))

        merged = pl.pallas_call(
            functools.partial(_xattn_kernel, heads=heads),
            grid=(t // tm, nj),
            in_specs=[pl.BlockSpec((tm, xw), lambda i, j: (i, (2 * cc + 2 * sw) // xw)),
                      pl.BlockSpec((nm, xw), lambda i, j: (i // tiles_per_seq, 0)),
                      pl.BlockSpec((nm, xw), lambda i, j: (i // tiles_per_seq, 1)),
                      *gate_specs(2),
                      tile_spec,
                      pl.BlockSpec((xw, unit), lambda i, j: (0, j))],
            out_specs=tile_spec,
            out_shape=jax.ShapeDtypeStruct((t, d), BF16),
            scratch_shapes=[pltpu.VMEM((tm, xw), BF16)],
            compiler_params=_params("arbitrary", "arbitrary"),
            name="xattn_branch",
        )(proj, kv, kv, proj, bg, gy, w_xattn_out[l].astype(BF16))

        wr = w_router[l].T
        wr_hi = wr.astype(BF16)
        wr_lo = (wr - wr_hi.astype(F32)).astype(BF16)
        tr = min(512, t)
        full = lambda shape: pl.BlockSpec(shape, lambda i: tuple(0 for _ in shape))
        kt_spec = pl.BlockSpec((TOP_K, tr), lambda i: (0, i))
        h, n2p, top_idx, probs, rank, counts = pl.pallas_call(
            functools.partial(_mix_route_kernel, experts=experts),
            grid=(t // tr,),
            in_specs=[pl.BlockSpec((tr, d), lambda i: (i, 0)),
                      pl.BlockSpec((tr, d), lambda i: (i, 0)),
                      full((d, d)), full((1, d)), full((experts, d)), full((experts, d)), full((experts, 1))],
            out_specs=[pl.BlockSpec((tr, d), lambda i: (i, 0)),
                       pl.BlockSpec((tr, d), lambda i: (i, 0)),
                       kt_spec, kt_spec, kt_spec,
                       full((experts, V7X_LANES))],
            out_shape=[jax.ShapeDtypeStruct((t, d), F32),
                       jax.ShapeDtypeStruct((t, d), F32),
                       jax.ShapeDtypeStruct((TOP_K, t), jnp.int32),
                       jax.ShapeDtypeStruct((TOP_K, t), F32),
                       jax.ShapeDtypeStruct((TOP_K, t), jnp.int32),
                       jax.ShapeDtypeStruct((experts, V7X_LANES), jnp.int32)],
            scratch_shapes=[pltpu.VMEM((experts, 1), F32)],
            compiler_params=_params("arbitrary"),
            name="mix_route",
        )(merged, h, w_out[l].astype(BF16), norm2_g[l].reshape(1, d), wr_hi, wr_lo, b_router[l].reshape(experts, 1))

        tile_log2 = 8
        te = 1 << tile_log2
        n_blocks = (t * TOP_K) // te + experts
        cap = n_blocks * te
        nb_pad = -(-n_blocks // V7X_LANES) * V7X_LANES
        tt = min(2048, t)
        dest, meta = pl.pallas_call(
            functools.partial(_route_table_kernel, experts=experts, tile_log2=tile_log2),
            grid=(t // tt,),
            in_specs=[pl.BlockSpec((experts, V7X_LANES), lambda i: (0, 0)),
                      pl.BlockSpec((TOP_K, tt), lambda i: (0, i)),
                      pl.BlockSpec((TOP_K, tt), lambda i: (0, i))],
            out_specs=[pl.BlockSpec((TOP_K, tt), lambda i: (0, i)),
                       pl.BlockSpec((V7X_SUBLANES, nb_pad), lambda i: (0, 0))],
            out_shape=[jax.ShapeDtypeStruct((TOP_K, t), jnp.int32),
                       jax.ShapeDtypeStruct((V7X_SUBLANES, nb_pad), jnp.int32)],
            compiler_params=_params("arbitrary"),
            name="route_table",
        )(counts, top_idx, rank)
        block_e, next_e = meta[0, :n_blocks], meta[1, :n_blocks]
        n_used, first_e = meta[2, :1], meta[3, :1]

        td = min(256, t)
        xp = pl.pallas_call(
            functools.partial(_dispatch_kernel, experts=experts, n_blocks=n_blocks, tile_log2=tile_log2),
            grid_spec=pltpu.PrefetchScalarGridSpec(
                num_scalar_prefetch=3,
                grid=(t // td,),
                in_specs=[pl.BlockSpec((TOP_K, td), lambda i, ps, cn, nu: (0, i), memory_space=pltpu.SMEM),
                          pl.BlockSpec((td, d), lambda i, ps, cn, nu: (i, 0))],
                out_specs=pl.BlockSpec(memory_space=pl.ANY),
                scratch_shapes=[pltpu.VMEM((te, d), F32),
                                pltpu.SemaphoreType.DMA(()), pltpu.SemaphoreType.DMA(())]),
            out_shape=jax.ShapeDtypeStruct((cap, d), F32),
            compiler_params=_params("arbitrary"),
            name="dispatch",
        )(meta[4, :experts], meta[5, :experts], n_used, dest, n2p)

        tn_up = min(1024, de)
        n_up = de // tn_up
        clamp = lambda m, nu: jnp.minimum(m, nu[0] - 1)
        bias_up = b_gate_up[l].reshape(experts, 1, 2 * de)
        hidden = pl.pallas_call(
            functools.partial(_expert_up_kernel, experts=experts, n_pass=n_up),
            grid_spec=pltpu.PrefetchScalarGridSpec(
                num_scalar_prefetch=4,
                grid=(n_up, n_blocks),
                in_specs=[pl.BlockSpec((te, d), lambda p, m, be, nx, nu, fe: (clamp(m, nu), 0)),
                          pl.BlockSpec((1, 1, tn_up), lambda p, m, be, nx, nu, fe: (be[clamp(m, nu)], 0, p)),
                          pl.BlockSpec((1, 1, tn_up),
                                       lambda p, m, be, nx, nu, fe: (be[clamp(m, nu)], 0, n_up + p)),
                          pl.BlockSpec(memory_space=pl.ANY)],
                out_specs=pl.BlockSpec((te, tn_up), lambda p, m, be, nx, nu, fe: (m, p)),
                scratch_shapes=[pltpu.VMEM((2, d, tn_up), F32),
                                pltpu.VMEM((d, tn_up), BF16), pltpu.VMEM((d, tn_up), BF16),
                                pltpu.SemaphoreType.DMA((2,))]),
            out_shape=jax.ShapeDtypeStruct((cap, de), BF16),
            compiler_params=_params("arbitrary", "arbitrary"),
            name="expert_up",
        )(block_e, next_e, n_used, first_e, xp, bias_up, bias_up, w_gate_up[l])

        tn_dn = min(2048, d)
        n_dn = d // tn_dn
        yp = pl.pallas_call(
            functools.partial(_expert_down_kernel, experts=experts, n_pass=n_dn),
            grid_spec=pltpu.PrefetchScalarGridSpec(
                num_scalar_prefetch=4,
                grid=(n_dn, n_blocks),
                in_specs=[pl.BlockSpec((te, de), lambda p, m, be, nx, nu, fe: (clamp(m, nu), 0)),
                          pl.BlockSpec((1, 1, tn_dn), lambda p, m, be, nx, nu, fe: (be[clamp(m, nu)], 0, p)),
                          pl.BlockSpec(memory_space=pl.ANY)],
                out_specs=pl.BlockSpec((te, tn_dn), lambda p, m, be, nx, nu, fe: (m, p)),
                scratch_shapes=[pltpu.VMEM((de, tn_dn), F32), pltpu.VMEM((de, tn_dn), BF16),
                                pltpu.SemaphoreType.DMA(())]),
            out_shape=jax.ShapeDtypeStruct((cap, d), F32),
            compiler_params=_params("arbitrary", "arbitrary"),
            name="expert_down",
        )(block_e, next_e, n_used, first_e, hidden, b_down[l].reshape(experts, 1, d), w_down[l])

        assert depth == 1
        tc = min(256, t)
        n_tc = t // tc
        h = pl.pallas_call(
            functools.partial(_combine_kernel, n_tiles=n_tc),
            grid=(n_tc,),
            in_specs=[pl.BlockSpec((TOP_K, tc), lambda i: (0, i), memory_space=pltpu.SMEM),
                      pl.BlockSpec((TOP_K, tc), lambda i: (0, jnp.minimum(i + 1, n_tc - 1)),
                                   memory_space=pltpu.SMEM),
                      pl.BlockSpec((tc, TOP_K), lambda i: (i, 0)),
                      pl.BlockSpec((tc, d), lambda i: (i, 0)),
                      pl.BlockSpec((1, d), lambda i: (0, 0)),
                      pl.BlockSpec(memory_space=pl.ANY)],
            out_specs=pl.BlockSpec((tc, d), lambda i: (i, 0)),
            out_shape=jax.ShapeDtypeStruct((t, d), F32),
            scratch_shapes=[pltpu.VMEM((2, TOP_K, tc, d), F32), pltpu.SemaphoreType.DMA((2,))],
            compiler_params=_params("arbitrary"),
            name="combine",
        )(dest, dest, probs.T, h, final_norm_g.reshape(1, d), yp)

    return h.reshape(b, s, d)
```

```python
import functools
import math

import jax
import jax.numpy as jnp
from jax import lax
from jax.experimental import pallas as pl
from jax.experimental.pallas import tpu as pltpu

EPS = 1e-5
TOP_K = 4
SWIGLU_LIMIT = 7.0
SWIGLU_ALPHA = 1.702
XATTN_HEAD_DIM_DIVISOR = 8

V7X_LANES = 128
V7X_SUBLANES = 8
V7X_VMEM_LIMIT_BYTES = 56 * 1024 * 1024

F32 = jnp.float32
BF16 = jnp.bfloat16


def _params(*sem):
    return pltpu.CompilerParams(dimension_semantics=sem, vmem_limit_bytes=V7X_VMEM_LIMIT_BYTES)


def _sigmoid(x):
    return 1.0 / (1.0 + jnp.exp(-x))


def _layer_norm(x, g, b):
    mu = jnp.mean(x, axis=-1, keepdims=True)
    xc = x - mu
    var = jnp.mean(xc * xc, axis=-1, keepdims=True)
    return xc * lax.rsqrt(var + EPS) * g + b


def _rms_norm(x, g):
    return x * lax.rsqrt(jnp.mean(x * x, axis=-1, keepdims=True) + EPS) * g


def _norm_matmul_kernel(x_ref, g_ref, w_ref, o_ref, xn_ref):
    @pl.when(pl.program_id(1) == 0)
    def _():
        xn_ref[...] = _rms_norm(x_ref[...], g_ref[...]).astype(BF16)

    o_ref[...] = jnp.dot(xn_ref[...], w_ref[...], preferred_element_type=F32).astype(o_ref.dtype)


def _norm_matmul(x2d, g, w_bf16, tm, tn):
    m, d = x2d.shape
    n = w_bf16.shape[1]
    return pl.pallas_call(
        _norm_matmul_kernel,
        grid=(m // tm, n // tn),
        in_specs=[
            pl.BlockSpec((tm, d), lambda i, j: (i, 0)),
            pl.BlockSpec((1, d), lambda i, j: (0, 0)),
            pl.BlockSpec((d, tn), lambda i, j: (0, j)),
        ],
        out_specs=pl.BlockSpec((tm, tn), lambda i, j: (i, j)),
        out_shape=jax.ShapeDtypeStruct((m, n), BF16),
        scratch_shapes=[pltpu.VMEM((tm, d), BF16)],
        compiler_params=_params("arbitrary", "arbitrary"),
        name="norm_matmul",
    )(x2d, g.reshape(1, d), w_bf16)


def _norm_matmul_split_kernel(x_ref, g_ref, w_ref, oa_ref, ob_ref, xn_ref):
    @pl.when(pl.program_id(1) == 0)
    def _():
        xn_ref[...] = _rms_norm(x_ref[...], g_ref[...]).astype(BF16)

    res = jnp.dot(xn_ref[...], w_ref[...], preferred_element_type=F32).astype(oa_ref.dtype)
    oa_ref[...] = res
    ob_ref[...] = res


def _norm_matmul_split(x2d, g, w_bf16, tm, tn, n_first):
    m, d = x2d.shape
    n_tiles = w_bf16.shape[1] // tn
    n_rest = n_tiles - n_first
    return pl.pallas_call(
        _norm_matmul_split_kernel,
        grid=(m // tm, n_tiles),
        in_specs=[
            pl.BlockSpec((tm, d), lambda i, j: (i, 0)),
            pl.BlockSpec((1, d), lambda i, j: (0, 0)),
            pl.BlockSpec((d, tn), lambda i, j: (0, j)),
        ],
        out_specs=[pl.BlockSpec((tm, tn), lambda i, j: (i, jnp.minimum(j, n_first))),
                   pl.BlockSpec((tm, tn), lambda i, j: (i, jnp.where(j >= n_first, j - n_first, n_rest)))],
        out_shape=[jax.ShapeDtypeStruct((m, (n_first + 1) * tn), BF16),
                   jax.ShapeDtypeStruct((m, (n_rest + 1) * tn), BF16)],
        scratch_shapes=[pltpu.VMEM((tm, d), BF16)],
        compiler_params=_params("arbitrary", "arbitrary"),
        name="norm_matmul_split",
    )(x2d, g.reshape(1, d), w_bf16)


CONV_HALO = 32
CONV_ROWS = 64
CONV_LANES = 256


def _conv_kernel(p_ref, q_ref, gl_ref, bg_ref, wdw_ref, bdw_ref, lg_ref, lb_ref, wo_ref, o_ref,
                 cp_ref, halo_ref, conv_ref, act_ref, *, tm, tiles_per_seq, kw):
    cc = p_ref.shape[1]
    back = (kw - 1) // V7X_SUBLANES * V7X_SUBLANES

    def branch():
        @pl.when(pl.program_id(0) % tiles_per_seq == 0)
        def _():
            halo_ref[...] = jnp.zeros_like(halo_ref)

        hg = p_ref[...].astype(F32) * _sigmoid(q_ref[...].astype(F32))
        halo = halo_ref[...]
        for r in range(V7X_SUBLANES):
            cp_ref[r, pl.ds(r, CONV_HALO), :] = halo
            cp_ref[r, pl.ds(CONV_HALO + r, tm), :] = hg
        halo_ref[...] = hg[tm - CONV_HALO:, :]

        for lc in range(cc // CONV_LANES):
            lanes = slice(lc * CONV_LANES, (lc + 1) * CONV_LANES)
            bias = bdw_ref[:, lanes]

            def body(rc, carry, lanes=lanes, bias=bias):
                base = rc * CONV_ROWS
                acc = jnp.zeros((CONV_ROWS // V7X_SUBLANES, V7X_SUBLANES, CONV_LANES), F32)
                for r in range(V7X_SUBLANES):
                    start = pl.multiple_of(base + (CONV_HALO - back), V7X_SUBLANES)
                    big = cp_ref[r, pl.ds(start, CONV_ROWS + back), lanes]
                    for k in range(kw):
                        delay = kw - 1 - k
                        if delay % V7X_SUBLANES != r:
                            continue
                        off = back - (delay - r)
                        win = big[off:off + CONV_ROWS].reshape(CONV_ROWS // V7X_SUBLANES, V7X_SUBLANES, CONV_LANES)
                        acc = acc + win * wdw_ref[k, :, lanes][None]
                conv_ref[pl.ds(pl.multiple_of(base, CONV_ROWS), CONV_ROWS), lanes] = (
                    acc.reshape(CONV_ROWS, CONV_LANES) + bias)
                return carry

            lax.fori_loop(0, tm // CONV_ROWS, body, 0)

        y = _layer_norm(conv_ref[...], lg_ref[...], lb_ref[...])
        act_ref[...] = (y * _sigmoid(y)).astype(BF16)

    branch()
    yo = jnp.dot(act_ref[...], wo_ref[...], preferred_element_type=F32)
    gate = _sigmoid(gl_ref[...].astype(F32) + bg_ref[...])
    o_ref[...] = (gate * yo).astype(o_ref.dtype)


def _gelu(x):
    return 0.5 * x * (1.0 + lax.erf(x * (1.0 / math.sqrt(2.0))))


def _sgu_kernel(u_ref, v_ref, gl_ref, bg_ref, acc_ref, lg_ref, lb_ref, ws_ref, bs_ref, wo_ref, o_ref,
                s_ref, *, tm, groups, chunk):
    sw = u_ref.shape[1]
    gd = sw // groups
    nchunk = tm // chunk

    def branch():
        u = _gelu(u_ref[...].astype(F32))
        v = _gelu(v_ref[...].astype(F32))
        vb = _layer_norm(v, lg_ref[...], lb_ref[...]).astype(BF16)
        row = lax.broadcasted_iota(jnp.int32, (chunk, chunk), 0)
        col = lax.broadcasted_iota(jnp.int32, (chunk, chunk), 1)
        causal = row >= col
        for g in range(groups):
            lanes = slice(g * gd, (g + 1) * gd)
            wg = jnp.where(causal, ws_ref[g], 0.0).astype(BF16)
            vcat = jnp.concatenate([vb[c * chunk:(c + 1) * chunk, lanes] for c in range(nchunk)], axis=1)
            og = jnp.dot(wg, vcat, preferred_element_type=F32)
            for c in range(nchunk):
                rows = slice(c * chunk, (c + 1) * chunk)
                sv = og[:, c * gd:(c + 1) * gd] + bs_ref[:, lanes]
                s_ref[rows, lanes] = (u[rows, lanes] * sv).astype(BF16)

    branch()
    yo = jnp.dot(s_ref[...], wo_ref[...], preferred_element_type=F32)
    gate = _sigmoid(gl_ref[...].astype(F32) + bg_ref[...])
    o_ref[...] = (acc_ref[...].astype(F32) + gate * yo).astype(o_ref.dtype)


def _xattn_kernel(q_ref, k_ref, v_ref, gl_ref, bg_ref, acc_ref, wo_ref, o_ref, att_ref, *, heads):
    xw = q_ref.shape[1]
    hd = xw // heads

    def branch():
        scale = hd ** -0.5
        for h in range(heads):
            lanes = slice(h * hd, (h + 1) * hd)
            s = lax.dot_general(q_ref[:, lanes], k_ref[:, lanes], (((1,), (1,)), ((), ())),
                                preferred_element_type=F32) * scale
            s = s - jnp.max(s, axis=-1, keepdims=True)
            e = jnp.exp(s)
            p = e / jnp.sum(e, axis=-1, keepdims=True)
            oh = jnp.dot(p.astype(BF16), v_ref[:, lanes], preferred_element_type=F32)
            att_ref[:, lanes] = oh.astype(BF16)

    branch()
    yo = jnp.dot(att_ref[...], wo_ref[...], preferred_element_type=F32)
    gate = _sigmoid(gl_ref[...].astype(F32) + bg_ref[...])
    o_ref[...] = (acc_ref[...].astype(F32) + gate * yo).astype(o_ref.dtype)


def _mix_route_kernel(m_ref, x_ref, wo_ref, g_ref, wrh_ref, wrl_ref, br_ref,
                      h_ref, n2_ref, idx_ref, prob_ref, rank_ref, cnt_ref, carry_ref, *, experts):
    tm = x_ref.shape[0]

    @pl.when(pl.program_id(0) == 0)
    def _():
        carry_ref[...] = jnp.zeros_like(carry_ref)

    h = x_ref[...] + jnp.dot(m_ref[...], wo_ref[...], preferred_element_type=F32)
    h_ref[...] = h
    n2 = _rms_norm(h, g_ref[...])
    n2_ref[...] = n2

    n_hi = n2.astype(BF16)
    n_lo = (n2 - n_hi.astype(F32)).astype(BF16)
    nt = (((1,), (1,)), ((), ()))
    lg = (lax.dot_general(wrh_ref[...], n_hi, nt, preferred_element_type=F32)
          + lax.dot_general(wrh_ref[...], n_lo, nt, preferred_element_type=F32)
          + lax.dot_general(wrl_ref[...], n_hi, nt, preferred_element_type=F32)
          + br_ref[...])

    eidx = lax.broadcasted_iota(jnp.int32, (experts, tm), 0)
    vals, onehots = [], []
    for k in range(TOP_K):
        mx = jnp.max(lg, axis=0, keepdims=True)
        ik = jnp.min(jnp.where(lg == mx, eidx, experts), axis=0, keepdims=True)
        sel = eidx == ik
        vals.append(mx)
        onehots.append(sel)
        idx_ref[k:k + 1, :] = ik
        lg = jnp.where(sel, -jnp.inf, lg)

    exps = [jnp.exp(v - vals[0]) for v in vals]
    denom = exps[0] + exps[1] + exps[2] + exps[3]
    for k in range(TOP_K):
        prob_ref[k:k + 1, :] = exps[k] / denom

    chosen = jnp.zeros((experts, tm), F32)
    for sel in onehots:
        chosen = chosen + sel.astype(F32)
    src = lax.broadcasted_iota(jnp.int32, (tm, tm), 0)
    dst = lax.broadcasted_iota(jnp.int32, (tm, tm), 1)
    before = (src < dst).astype(BF16)
    prior = jnp.dot(chosen.astype(BF16), before, preferred_element_type=F32) + carry_ref[...]
    for k in range(TOP_K):
        rk = jnp.sum(jnp.where(onehots[k], prior, 0.0), axis=0, keepdims=True)
        rank_ref[k:k + 1, :] = rk.astype(jnp.int32)
    carry_ref[...] = carry_ref[...] + jnp.sum(chosen, axis=1, keepdims=True)
    cnt_ref[...] = jnp.broadcast_to(carry_ref[...], cnt_ref.shape).astype(jnp.int32)


def _route_table_kernel(cnt_ref, idx_ref, rank_ref, dest_ref, meta_ref, *, experts, tile_log2):
    lanes = cnt_ref.shape[1]
    cnt = cnt_ref[...]
    padded = ((cnt + ((1 << tile_log2) - 1)) >> tile_log2) << tile_log2
    padded_f = padded.astype(F32)
    row = lax.broadcasted_iota(jnp.int32, (experts, lanes), 0)
    lane = lax.broadcasted_iota(jnp.int32, (experts, lanes), 1)
    padded_row = jnp.sum(jnp.where(row == lane, padded_f, 0.0), axis=0, keepdims=True)
    pstart_col = jnp.sum(jnp.where(lane < row, padded_row, 0.0), axis=1, keepdims=True)
    pend_col = pstart_col + padded_f[:, 0:1]

    tt = idx_ref.shape[1]
    eidx = lax.broadcasted_iota(jnp.int32, (experts, tt), 0)
    for k in range(TOP_K):
        sel = eidx == idx_ref[k:k + 1, :]
        start = jnp.sum(jnp.where(sel, pstart_col, 0.0), axis=0, keepdims=True)
        dest_ref[k:k + 1, :] = rank_ref[k:k + 1, :] + start.astype(jnp.int32)

    nb = meta_ref.shape[1]
    tile_start = (lax.broadcasted_iota(jnp.int32, (experts, nb), 1) << tile_log2).astype(F32)
    ecol = lax.broadcasted_iota(jnp.int32, (experts, nb), 0)
    be = jnp.minimum(jnp.sum((pend_col <= tile_start).astype(F32), axis=0, keepdims=True).astype(jnp.int32),
                     experts - 1)
    nonempty = padded_f[:, 0:1] > 0.0
    nxt = jnp.min(jnp.where(nonempty & (ecol > be), ecol, experts), axis=0, keepdims=True)
    first = jnp.min(jnp.where(nonempty, ecol, experts), axis=0, keepdims=True)
    total = jnp.sum(padded_f, axis=0, keepdims=True)[:, 0:1].astype(jnp.int32) >> tile_log2
    meta_ref[...] = jnp.zeros_like(meta_ref)
    meta_ref[0:1, :] = be
    meta_ref[1:2, :] = nxt
    meta_ref[2:3, :] = jnp.broadcast_to(total, (1, nb))
    meta_ref[3:4, :] = first
    pstart_row = jnp.sum(jnp.where(row < lane, padded_f, 0.0), axis=0, keepdims=True)
    cnt_row = jnp.sum(jnp.where(row == lane, cnt.astype(F32), 0.0), axis=0, keepdims=True)
    meta_ref[4:5, 0:lanes] = pstart_row.astype(jnp.int32)
    meta_ref[5:6, 0:lanes] = cnt_row.astype(jnp.int32)


def _dispatch_kernel(ps_ref, cnt_ref, nu_ref, dest_ref, n2_ref, xp_ref, zero_ref, sem, zsem,
                     *, experts, n_blocks, tile_log2):
    tm = n2_ref.shape[0]
    te = 1 << tile_log2

    @pl.when(pl.program_id(0) == 0)
    def _():
        zero_ref[...] = jnp.zeros_like(zero_ref)

        def pad_rows(e):
            first = ps_ref[e] + cnt_ref[e]
            return first, ((cnt_ref[e] + (te - 1)) >> tile_log2 << tile_log2) - cnt_ref[e]

        def pad_copy(row):
            return pltpu.make_async_copy(zero_ref.at[pl.ds(0, 1)], xp_ref.at[pl.ds(row, 1)], zsem)

        def tile_copy(b):
            return pltpu.make_async_copy(zero_ref, xp_ref.at[pl.ds(pl.multiple_of(b * te, te), te)], zsem)

        def for_each_fill(fn_row, fn_tile):
            def per_expert(e, carry):
                first, n = pad_rows(e)
                lax.fori_loop(0, n, lambda r, c: (fn_row(first + r), c)[1], 0)
                return carry

            lax.fori_loop(0, experts, per_expert, 0)
            lax.fori_loop(nu_ref[0], n_blocks, lambda b, c: (fn_tile(b), c)[1], 0)

        for_each_fill(lambda row: pad_copy(row).start(), lambda b: tile_copy(b).start())
        for_each_fill(lambda row: pad_copy(row).wait(), lambda b: tile_copy(b).wait())

    def row_copy(t, d):
        return pltpu.make_async_copy(n2_ref.at[pl.ds(t, 1)], xp_ref.at[pl.ds(d, 1)], sem)

    def issue(t, carry):
        for k in range(TOP_K):
            row_copy(t, dest_ref[k, t]).start()
        return carry

    lax.fori_loop(0, tm, issue, 0)

    for k in range(TOP_K):
        pltpu.make_async_copy(n2_ref, xp_ref.at[pl.ds(0, tm)], sem).wait()


def _stream_expert_weights(be_ref, nx_ref, fe_ref, copies, load, *, experts, n_pass):
    p = pl.program_id(0)
    m = pl.program_id(1)
    e = be_ref[m]

    @pl.when((m == 0) | (e != be_ref[jnp.maximum(m - 1, 0)]))
    def _():
        @pl.when((m == 0) & (p == 0))
        def _():
            for c in copies(e, p):
                c.start()

        for c in copies(e, p):
            c.wait()
        load()
        same_pass = nx_ref[m] < experts
        e2 = jnp.where(same_pass, nx_ref[m], fe_ref[0])
        p2 = jnp.where(same_pass, p, p + 1)

        @pl.when(p2 < n_pass)
        def _():
            for c in copies(e2, p2):
                c.start(priority=1)


WEIGHT_CAST_ROWS = 128


def _cast_rows(dst_ref, src_ref):
    def body(i, carry):
        rows = pl.ds(pl.multiple_of(i * WEIGHT_CAST_ROWS, WEIGHT_CAST_ROWS), WEIGHT_CAST_ROWS)
        dst_ref[rows, :] = src_ref[rows, :].astype(BF16)
        return carry

    lax.fori_loop(0, dst_ref.shape[0] // WEIGHT_CAST_ROWS, body, 0)


def _expert_up_kernel(be_ref, nx_ref, nu_ref, fe_ref, x_ref, bg_ref, bu_ref, w_hbm, o_ref,
                      stage_ref, wgb_ref, wub_ref, sem, *, experts, n_pass):
    m = pl.program_id(1)
    tn = o_ref.shape[1]
    de = n_pass * tn

    def copies(e, p):
        return [pltpu.make_async_copy(w_hbm.at[e, :, pl.ds(pl.multiple_of(half * de + p * tn, tn), tn)],
                                      stage_ref.at[half], sem.at[half]) for half in range(2)]

    def load():
        _cast_rows(wgb_ref, stage_ref.at[0])
        _cast_rows(wub_ref, stage_ref.at[1])

    @pl.when(m < nu_ref[0])
    def _():
        _stream_expert_weights(be_ref, nx_ref, fe_ref, copies, load, experts=experts, n_pass=n_pass)
        x = x_ref[...].astype(BF16)
        g = jnp.dot(x, wgb_ref[...], preferred_element_type=F32) + bg_ref[0]
        u = jnp.dot(x, wub_ref[...], preferred_element_type=F32) + bu_ref[0]
        g = jnp.minimum(g, SWIGLU_LIMIT)
        u = jnp.clip(u, -SWIGLU_LIMIT, SWIGLU_LIMIT)
        o_ref[...] = ((u + 1.0) * (g * _sigmoid(SWIGLU_ALPHA * g))).astype(o_ref.dtype)

    @pl.when(m >= nu_ref[0])
    def _():
        o_ref[...] = jnp.zeros_like(o_ref)


def _expert_down_kernel(be_ref, nx_ref, nu_ref, fe_ref, h_ref, b_ref, w_hbm, o_ref, stage_ref, wb_ref, sem,
                        *, experts, n_pass):
    m = pl.program_id(1)
    tn = o_ref.shape[1]

    def copies(e, p):
        return [pltpu.make_async_copy(w_hbm.at[e, :, pl.ds(pl.multiple_of(p * tn, tn), tn)], stage_ref, sem)]

    def load():
        _cast_rows(wb_ref, stage_ref)

    @pl.when(m < nu_ref[0])
    def _():
        _stream_expert_weights(be_ref, nx_ref, fe_ref, copies, load, experts=experts, n_pass=n_pass)
        o_ref[...] = jnp.dot(h_ref[...], wb_ref[...], preferred_element_type=F32) + b_ref[0]

    @pl.when(m >= nu_ref[0])
    def _():
        o_ref[...] = jnp.zeros_like(o_ref)


def _combine_kernel(dest_ref, dnext_ref, p_ref, h_ref, g_ref, yp_ref, o_ref, rows_ref, sem, *, n_tiles):
    i = pl.program_id(0)
    tm = h_ref.shape[0]
    slot = i % 2

    def row_copy(idx_ref, buf, t, k):
        return pltpu.make_async_copy(yp_ref.at[pl.ds(idx_ref[k, t], 1)], rows_ref.at[buf, k, pl.ds(t, 1)],
                                     sem.at[buf])

    def for_each_row(idx_ref, buf, fn):
        def body(t, carry):
            for k in range(TOP_K):
                fn(row_copy(idx_ref, buf, t, k), k)
            return carry

        lax.fori_loop(0, tm, body, 0)

    start = lambda c, k: c.start()

    @pl.when(i == 0)
    def _():
        for_each_row(dest_ref, 0, start)

    @pl.when(i + 1 < n_tiles)
    def _():
        for_each_row(dnext_ref, 1 - slot, start)

    for k in range(TOP_K):
        pltpu.make_async_copy(yp_ref.at[pl.ds(0, tm)], rows_ref.at[slot, k], sem.at[slot]).wait()

    y = h_ref[...]
    for k in range(TOP_K):
        y = y + p_ref[:, k:k + 1] * rows_ref[slot, k]
    o_ref[...] = _rms_norm(y, g_ref[...])


def kernel(x, mem, norm1_g, w_in, b_gate, w_dw, b_dw, conv_ln_g, conv_ln_b, w_conv_out, sgu_ln_g, sgu_ln_b,
           w_spatial, b_spatial, w_sgu_out, mem_norm_g, w_kv, w_xattn_out, w_out, norm2_g, w_router, b_router,
           w_gate_up, b_gate_up, w_down, b_down, final_norm_g):
    b, s, d = x.shape
    depth = norm1_g.shape[0]
    t = b * s
    nm = mem.shape[1]
    kw, cc = w_dw.shape[1], w_dw.shape[2]
    sw = w_sgu_out.shape[1]
    groups, chunk = w_spatial.shape[1], w_spatial.shape[2]
    xw = w_xattn_out.shape[1]
    heads = xw // (d // XATTN_HEAD_DIM_DIVISOR)
    experts = w_router.shape[2]
    de = w_down.shape[2]
    in_cols = w_in.shape[2]

    unit = math.gcd(math.gcd(cc, sw), math.gcd(xw, d))
    assert in_cols == 2 * cc + 2 * sw + xw + 3 * d and in_cols % unit == 0
    assert (sw // groups) % V7X_LANES == 0 and kw - 1 <= CONV_HALO and cc % CONV_LANES == 0
    assert (2 * cc) % sw == 0 and (2 * cc + 2 * sw) % xw == 0 and d % 2 == 0

    tm = min(512, s)
    assert s % tm == 0 and tm % chunk == 0 and tm % CONV_ROWS == 0
    tiles_per_seq = s // tm
    gate_col0 = (2 * cc + 2 * sw + xw) // unit

    h = x.reshape(t, d)
    for l in range(depth):
        proj, gates = _norm_matmul_split(h, norm1_g[l], w_in[l].astype(BF16), min(1024, t), unit, gate_col0)
        kv = _norm_matmul(mem.reshape(b * nm, d), mem_norm_g[l], w_kv[l].astype(BF16), min(512, b * nm), xw)
        bg = b_gate[l].reshape(1, 3 * d)

        def gate_specs(branch):
            return [pl.BlockSpec((tm, d), lambda i: (i, branch)), pl.BlockSpec((1, d), lambda i: (0, branch))]

        tile_spec = pl.BlockSpec((tm, d), lambda i: (i, 0))
        row_vec = lambda n: pl.BlockSpec((1, n), lambda i: (0, 0))
        whole = lambda shape: pl.BlockSpec(shape, lambda i: tuple(0 for _ in shape))

        wdw_b = jnp.broadcast_to(w_dw[l][:, None, :], (kw, V7X_SUBLANES, cc))
        gy = pl.pallas_call(
            functools.partial(_conv_kernel, tm=tm, tiles_per_seq=tiles_per_seq, kw=kw),
            grid=(t // tm,),
            in_specs=[pl.BlockSpec((tm, cc), lambda i: (i, 0)),
                      pl.BlockSpec((tm, cc), lambda i: (i, 1)),
                      *gate_specs(0),
                      whole((kw, V7X_SUBLANES, cc)),
                      row_vec(cc), row_vec(cc), row_vec(cc),
                      whole((cc, d))],
            out_specs=tile_spec,
            out_shape=jax.ShapeDtypeStruct((t, d), BF16),
            scratch_shapes=[pltpu.VMEM((V7X_SUBLANES, tm + CONV_HALO + V7X_SUBLANES, cc), F32),
                            pltpu.VMEM((CONV_HALO, cc), F32),
                            pltpu.VMEM((tm, cc), F32),
                            pltpu.VMEM((tm, cc), BF16)],
            compiler_params=_params("arbitrary"),
            name="conv_branch",
        )(proj, proj, gates, bg, wdw_b, b_dw[l].reshape(1, cc), conv_ln_g[l].reshape(1, cc),
          conv_ln_b[l].reshape(1, cc), w_conv_out[l].astype(BF16))

        bs_full = jnp.repeat(b_spatial[l].T, sw // groups, axis=1)
        gy = pl.pallas_call(
            functools.partial(_sgu_kernel, tm=tm, groups=groups, chunk=chunk),
            grid=(t // tm,),
            in_specs=[pl.BlockSpec((tm, sw), lambda i: (i, 2 * cc // sw)),
                      pl.BlockSpec((tm, sw), lambda i: (i, 2 * cc // sw + 1)),
                      *gate_specs(1),
                      tile_spec,
                      row_vec(sw), row_vec(sw),
                      whole((groups, chunk, chunk)),
                      whole((chunk, sw)),
                      whole((sw, d))],
            out_specs=tile_spec,
            out_shape=jax.ShapeDtypeStruct((t, d), BF16),
            scratch_shapes=[pltpu.VMEM((tm, sw), BF16)],
            compiler_params=_params("arbitrary"),
            name="sgu_branch",
        )(proj, proj, gates, bg, gy, sgu_ln_g[l].reshape(1, sw), sgu_ln_b[l].reshape(1, sw),
          w_spatial[l], bs_full, w_sgu_out[l].astype(BF16))

        merged = pl.pallas_call(
            functools.partial(_xattn_kernel, heads=heads),
            grid=(t // tm,),
            in_specs=[pl.BlockSpec((tm, xw), lambda i: (i, (2 * cc + 2 * sw) // xw)),
                      pl.BlockSpec((nm, xw), lambda i: (i // tiles_per_seq, 0)),
                      pl.BlockSpec((nm, xw), lambda i: (i // tiles_per_seq, 1)),
                      *gate_specs(2),
                      tile_spec,
                      whole((xw, d))],
            out_specs=tile_spec,
            out_shape=jax.ShapeDtypeStruct((t, d), BF16),
            scratch_shapes=[pltpu.VMEM((tm, xw), BF16)],
            compiler_params=_params("arbitrary"),
            name="xattn_branch",
        )(proj, kv, kv, gates, bg, gy, w_xattn_out[l].astype(BF16))

        wr = w_router[l].T
        wr_hi = wr.astype(BF16)
        wr_lo = (wr - wr_hi.astype(F32)).astype(BF16)
        tr = min(512, t)
        full = lambda shape: pl.BlockSpec(shape, lambda i: tuple(0 for _ in shape))
        kt_spec = pl.BlockSpec((TOP_K, tr), lambda i: (0, i))
        h, n2p, top_idx, probs, rank, counts = pl.pallas_call(
            functools.partial(_mix_route_kernel, experts=experts),
            grid=(t // tr,),
            in_specs=[pl.BlockSpec((tr, d), lambda i: (i, 0)),
                      pl.BlockSpec((tr, d), lambda i: (i, 0)),
                      full((d, d)), full((1, d)), full((experts, d)), full((experts, d)), full((experts, 1))],
            out_specs=[pl.BlockSpec((tr, d), lambda i: (i, 0)),
                       pl.BlockSpec((tr, d), lambda i: (i, 0)),
                       kt_spec, kt_spec, kt_spec,
                       full((experts, V7X_LANES))],
            out_shape=[jax.ShapeDtypeStruct((t, d), F32),
                       jax.ShapeDtypeStruct((t, d), F32),
                       jax.ShapeDtypeStruct((TOP_K, t), jnp.int32),
                       jax.ShapeDtypeStruct((TOP_K, t), F32),
                       jax.ShapeDtypeStruct((TOP_K, t), jnp.int32),
                       jax.ShapeDtypeStruct((experts, V7X_LANES), jnp.int32)],
            scratch_shapes=[pltpu.VMEM((experts, 1), F32)],
            compiler_params=_params("arbitrary"),
            name="mix_route",
        )(merged, h, w_out[l].astype(BF16), norm2_g[l].reshape(1, d), wr_hi, wr_lo, b_router[l].reshape(experts, 1))

        tile_log2 = 8
        te = 1 << tile_log2
        n_blocks = (t * TOP_K) // te + experts
        cap = n_blocks * te
        nb_pad = -(-n_blocks // V7X_LANES) * V7X_LANES
        tt = min(2048, t)
        dest, meta = pl.pallas_call(
            functools.partial(_route_table_kernel, experts=experts, tile_log2=tile_log2),
            grid=(t // tt,),
            in_specs=[pl.BlockSpec((experts, V7X_LANES), lambda i: (0, 0)),
                      pl.BlockSpec((TOP_K, tt), lambda i: (0, i)),
                      pl.BlockSpec((TOP_K, tt), lambda i: (0, i))],
            out_specs=[pl.BlockSpec((TOP_K, tt), lambda i: (0, i)),
                       pl.BlockSpec((V7X_SUBLANES, nb_pad), lambda i: (0, 0))],
            out_shape=[jax.ShapeDtypeStruct((TOP_K, t), jnp.int32),
                       jax.ShapeDtypeStruct((V7X_SUBLANES, nb_pad), jnp.int32)],
            compiler_params=_params("arbitrary"),
            name="route_table",
        )(counts, top_idx, rank)
        block_e, next_e = meta[0, :n_blocks], meta[1, :n_blocks]
        n_used, first_e = meta[2, :1], meta[3, :1]

        td = min(256, t)
        xp = pl.pallas_call(
            functools.partial(_dispatch_kernel, experts=experts, n_blocks=n_blocks, tile_log2=tile_log2),
            grid_spec=pltpu.PrefetchScalarGridSpec(
                num_scalar_prefetch=3,
                grid=(t // td,),
                in_specs=[pl.BlockSpec((TOP_K, td), lambda i, ps, cn, nu: (0, i), memory_space=pltpu.SMEM),
                          pl.BlockSpec((td, d), lambda i, ps, cn, nu: (i, 0))],
                out_specs=pl.BlockSpec(memory_space=pl.ANY),
                scratch_shapes=[pltpu.VMEM((te, d), F32),
                                pltpu.SemaphoreType.DMA(()), pltpu.SemaphoreType.DMA(())]),
            out_shape=jax.ShapeDtypeStruct((cap, d), F32),
            compiler_params=_params("arbitrary"),
            name="dispatch",
        )(meta[4, :experts], meta[5, :experts], n_used, dest, n2p)

        tn_up = min(1024, de)
        n_up = de // tn_up
        clamp = lambda m, nu: jnp.minimum(m, nu[0] - 1)
        bias_up = b_gate_up[l].reshape(experts, 1, 2 * de)
        hidden = pl.pallas_call(
            functools.partial(_expert_up_kernel, experts=experts, n_pass=n_up),
            grid_spec=pltpu.PrefetchScalarGridSpec(
                num_scalar_prefetch=4,
                grid=(n_up, n_blocks),
                in_specs=[pl.BlockSpec((te, d), lambda p, m, be, nx, nu, fe: (clamp(m, nu), 0)),
                          pl.BlockSpec((1, 1, tn_up), lambda p, m, be, nx, nu, fe: (be[clamp(m, nu)], 0, p)),
                          pl.BlockSpec((1, 1, tn_up),
                                       lambda p, m, be, nx, nu, fe: (be[clamp(m, nu)], 0, n_up + p)),
                          pl.BlockSpec(memory_space=pl.ANY)],
                out_specs=pl.BlockSpec((te, tn_up), lambda p, m, be, nx, nu, fe: (m, p)),
                scratch_shapes=[pltpu.VMEM((2, d, tn_up), F32),
                                pltpu.VMEM((d, tn_up), BF16), pltpu.VMEM((d, tn_up), BF16),
                                pltpu.SemaphoreType.DMA((2,))]),
            out_shape=jax.ShapeDtypeStruct((cap, de), BF16),
            compiler_params=_params("arbitrary", "arbitrary"),
            name="expert_up",
        )(block_e, next_e, n_used, first_e, xp, bias_up, bias_up, w_gate_up[l])

        tn_dn = min(2048, d)
        n_dn = d // tn_dn
        yp = pl.pallas_call(
            functools.partial(_expert_down_kernel, experts=experts, n_pass=n_dn),
            grid_spec=pltpu.PrefetchScalarGridSpec(
                num_scalar_prefetch=4,
                grid=(n_dn, n_blocks),
                in_specs=[pl.BlockSpec((te, de), lambda p, m, be, nx, nu, fe: (clamp(m, nu), 0)),
                          pl.BlockSpec((1, 1, tn_dn), lambda p, m, be, nx, nu, fe: (be[clamp(m, nu)], 0, p)),
                          pl.BlockSpec(memory_space=pl.ANY)],
                out_specs=pl.BlockSpec((te, tn_dn), lambda p, m, be, nx, nu, fe: (m, p)),
                scratch_shapes=[pltpu.VMEM((de, tn_dn), F32), pltpu.VMEM((de, tn_dn), BF16),
                                pltpu.SemaphoreType.DMA(())]),
            out_shape=jax.ShapeDtypeStruct((cap, d), F32),
            compiler_params=_params("arbitrary", "arbitrary"),
            name="expert_down",
        )(block_e, next_e, n_used, first_e, hidden, b_down[l].reshape(experts, 1, d), w_down[l])

        assert depth == 1
        tc = min(256, t)
        n_tc = t // tc
        h = pl.pallas_call(
            functools.partial(_combine_kernel, n_tiles=n_tc),
            grid=(n_tc,),
            in_specs=[pl.BlockSpec((TOP_K, tc), lambda i: (0, i), memory_space=pltpu.SMEM),
                      pl.BlockSpec((TOP_K, tc), lambda i: (0, jnp.minimum(i + 1, n_tc - 1)),
                                   memory_space=pltpu.SMEM),
                      pl.BlockSpec((tc, TOP_K), lambda i: (i, 0)),
                      pl.BlockSpec((tc, d), lambda i: (i, 0)),
                      pl.BlockSpec((1, d), lambda i: (0, 0)),
                      pl.BlockSpec(memory_space=pl.ANY)],
            out_specs=pl.BlockSpec((tc, d), lambda i: (i, 0)),
            out_shape=jax.ShapeDtypeStruct((t, d), F32),
            scratch_shapes=[pltpu.VMEM((2, TOP_K, tc, d), F32), pltpu.SemaphoreType.DMA((2,))],
            compiler_params=_params("arbitrary"),
            name="combine",
        )(dest, dest, probs.T, h, final_norm_g.reshape(1, d), yp)

    return h.reshape(b, s, d)
```

```python
import functools
import math

import jax
import jax.numpy as jnp
from jax import lax
from jax.experimental import pallas as pl
from jax.experimental.pallas import tpu as pltpu

EPS = 1e-5
TOP_K = 4
SWIGLU_LIMIT = 7.0
SWIGLU_ALPHA = 1.702
XATTN_HEAD_DIM_DIVISOR = 8

V7X_LANES = 128
V7X_SUBLANES = 8
V7X_VMEM_LIMIT_BYTES = 56 * 1024 * 1024

F32 = jnp.float32
BF16 = jnp.bfloat16


def _params(*sem):
    return pltpu.CompilerParams(dimension_semantics=sem, vmem_limit_bytes=V7X_VMEM_LIMIT_BYTES)


def _sigmoid(x):
    return 1.0 / (1.0 + jnp.exp(-x))


def _layer_norm(x, g, b):
    mu = jnp.mean(x, axis=-1, keepdims=True)
    xc = x - mu
    var = jnp.mean(xc * xc, axis=-1, keepdims=True)
    return xc * lax.rsqrt(var + EPS) * g + b


def _rms_norm(x, g):
    return x * lax.rsqrt(jnp.mean(x * x, axis=-1, keepdims=True) + EPS) * g


def _norm_matmul_kernel(x_ref, g_ref, w_ref, o_ref, xn_ref):
    @pl.when(pl.program_id(1) == 0)
    def _():
        xn_ref[...] = _rms_norm(x_ref[...], g_ref[...]).astype(BF16)

    o_ref[...] = jnp.dot(xn_ref[...], w_ref[...], preferred_element_type=F32).astype(o_ref.dtype)


def _norm_matmul(x2d, g, w_bf16, tm, tn):
    m, d = x2d.shape
    n = w_bf16.shape[1]
    return pl.pallas_call(
        _norm_matmul_kernel,
        grid=(m // tm, n // tn),
        in_specs=[
            pl.BlockSpec((tm, d), lambda i, j: (i, 0)),
            pl.BlockSpec((1, d), lambda i, j: (0, 0)),
            pl.BlockSpec((d, tn), lambda i, j: (0, j)),
        ],
        out_specs=pl.BlockSpec((tm, tn), lambda i, j: (i, j)),
        out_shape=jax.ShapeDtypeStruct((m, n), BF16),
        scratch_shapes=[pltpu.VMEM((tm, d), BF16)],
        compiler_params=_params("arbitrary", "arbitrary"),
        name="norm_matmul",
    )(x2d, g.reshape(1, d), w_bf16)


def _norm_matmul_split_kernel(x_ref, g_ref, w_ref, oa_ref, ob_ref, xn_ref):
    @pl.when(pl.program_id(1) == 0)
    def _():
        xn_ref[...] = _rms_norm(x_ref[...], g_ref[...]).astype(BF16)

    res = jnp.dot(xn_ref[...], w_ref[...], preferred_element_type=F32).astype(oa_ref.dtype)
    oa_ref[...] = res
    ob_ref[...] = res


def _norm_matmul_split(x2d, g, w_bf16, tm, tn, n_first):
    m, d = x2d.shape
    n_tiles = w_bf16.shape[1] // tn
    n_rest = n_tiles - n_first
    return pl.pallas_call(
        _norm_matmul_split_kernel,
        grid=(m // tm, n_tiles),
        in_specs=[
            pl.BlockSpec((tm, d), lambda i, j: (i, 0)),
            pl.BlockSpec((1, d), lambda i, j: (0, 0)),
            pl.BlockSpec((d, tn), lambda i, j: (0, j)),
        ],
        out_specs=[pl.BlockSpec((tm, tn), lambda i, j: (i, jnp.minimum(j, n_first))),
                   pl.BlockSpec((tm, tn), lambda i, j: (i, jnp.where(j >= n_first, j - n_first, n_rest)))],
        out_shape=[jax.ShapeDtypeStruct((m, (n_first + 1) * tn), BF16),
                   jax.ShapeDtypeStruct((m, (n_rest + 1) * tn), BF16)],
        scratch_shapes=[pltpu.VMEM((tm, d), BF16)],
        compiler_params=_params("arbitrary", "arbitrary"),
        name="norm_matmul_split",
    )(x2d, g.reshape(1, d), w_bf16)


CONV_HALO = 32
CONV_ROWS = 64
CONV_LANES = 256


def _conv_kernel(p_ref, q_ref, gl_ref, bg_ref, wdw_ref, bdw_ref, lg_ref, lb_ref, wo_ref, o_ref,
                 cp_ref, halo_ref, conv_ref, act_ref, *, tm, tiles_per_seq, kw):
    cc = p_ref.shape[1]
    back = (kw - 1) // V7X_SUBLANES * V7X_SUBLANES

    def branch():
        @pl.when(pl.program_id(0) % tiles_per_seq == 0)
        def _():
            halo_ref[...] = jnp.zeros_like(halo_ref)

        hg = p_ref[...].astype(F32) * _sigmoid(q_ref[...].astype(F32))
        halo = halo_ref[...]
        for r in range(V7X_SUBLANES):
            cp_ref[r, pl.ds(r, CONV_HALO), :] = halo
            cp_ref[r, pl.ds(CONV_HALO + r, tm), :] = hg
        halo_ref[...] = hg[tm - CONV_HALO:, :]

        for lc in range(cc // CONV_LANES):
            lanes = slice(lc * CONV_LANES, (lc + 1) * CONV_LANES)
            bias = bdw_ref[:, lanes]

            def body(rc, carry, lanes=lanes, bias=bias):
                base = rc * CONV_ROWS
                acc = jnp.zeros((CONV_ROWS // V7X_SUBLANES, V7X_SUBLANES, CONV_LANES), F32)
                for r in range(V7X_SUBLANES):
                    start = pl.multiple_of(base + (CONV_HALO - back), V7X_SUBLANES)
                    big = cp_ref[r, pl.ds(start, CONV_ROWS + back), lanes]
                    for k in range(kw):
                        delay = kw - 1 - k
                        if delay % V7X_SUBLANES != r:
                            continue
                        off = back - (delay - r)
                        win = big[off:off + CONV_ROWS].reshape(CONV_ROWS // V7X_SUBLANES, V7X_SUBLANES, CONV_LANES)
                        acc = acc + win * wdw_ref[k, :, lanes][None]
                conv_ref[pl.ds(pl.multiple_of(base, CONV_ROWS), CONV_ROWS), lanes] = (
                    acc.reshape(CONV_ROWS, CONV_LANES) + bias)
                return carry

            lax.fori_loop(0, tm // CONV_ROWS, body, 0)

        y = _layer_norm(conv_ref[...], lg_ref[...], lb_ref[...])
        act_ref[...] = (y * _sigmoid(y)).astype(BF16)

    branch()
    yo = jnp.dot(act_ref[...], wo_ref[...], preferred_element_type=F32)
    gate = _sigmoid(gl_ref[...].astype(F32) + bg_ref[...])
    o_ref[...] = (gate * yo).astype(o_ref.dtype)


def _gelu(x):
    return 0.5 * x * (1.0 + lax.erf(x * (1.0 / math.sqrt(2.0))))


def _sgu_kernel(u_ref, v_ref, gl_ref, bg_ref, acc_ref, lg_ref, lb_ref, ws_ref, bs_ref, wo_ref, o_ref,
                s_ref, *, tm, groups, chunk):
    sw = u_ref.shape[1]
    gd = sw // groups
    nchunk = tm // chunk

    def branch():
        u = _gelu(u_ref[...].astype(F32))
        v = _gelu(v_ref[...].astype(F32))
        vb = _layer_norm(v, lg_ref[...], lb_ref[...]).astype(BF16)
        row = lax.broadcasted_iota(jnp.int32, (chunk, chunk), 0)
        col = lax.broadcasted_iota(jnp.int32, (chunk, chunk), 1)
        causal = row >= col
        for g in range(groups):
            lanes = slice(g * gd, (g + 1) * gd)
            wg = jnp.where(causal, ws_ref[g], 0.0).astype(BF16)
            vcat = jnp.concatenate([vb[c * chunk:(c + 1) * chunk, lanes] for c in range(nchunk)], axis=1)
            og = jnp.dot(wg, vcat, preferred_element_type=F32)
            for c in range(nchunk):
                rows = slice(c * chunk, (c + 1) * chunk)
                sv = og[:, c * gd:(c + 1) * gd] + bs_ref[:, lanes]
                s_ref[rows, lanes] = (u[rows, lanes] * sv).astype(BF16)

    branch()
    yo = jnp.dot(s_ref[...], wo_ref[...], preferred_element_type=F32)
    gate = _sigmoid(gl_ref[...].astype(F32) + bg_ref[...])
    o_ref[...] = (acc_ref[...].astype(F32) + gate * yo).astype(o_ref.dtype)


def _xattn_kernel(q_ref, k_ref, v_ref, gl_ref, bg_ref, acc_ref, wo_ref, o_ref, att_ref, *, heads):
    xw = q_ref.shape[1]
    hd = xw // heads

    def branch():
        scale = hd ** -0.5
        for h in range(heads):
            lanes = slice(h * hd, (h + 1) * hd)
            s = lax.dot_general(q_ref[:, lanes], k_ref[:, lanes], (((1,), (1,)), ((), ())),
                                preferred_element_type=F32) * scale
            s = s - jnp.max(s, axis=-1, keepdims=True)
            e = jnp.exp(s)
            p = e / jnp.sum(e, axis=-1, keepdims=True)
            oh = jnp.dot(p.astype(BF16), v_ref[:, lanes], preferred_element_type=F32)
            att_ref[:, lanes] = oh.astype(BF16)

    branch()
    yo = jnp.dot(att_ref[...], wo_ref[...], preferred_element_type=F32)
    gate = _sigmoid(gl_ref[...].astype(F32) + bg_ref[...])
    o_ref[...] = (acc_ref[...].astype(F32) + gate * yo).astype(o_ref.dtype)


def _mix_route_kernel(m_ref, x_ref, wo_ref, g_ref, wrh_ref, wrl_ref, br_ref,
                      h_ref, n2_ref, idx_ref, prob_ref, rank_ref, cnt_ref, carry_ref, *, experts):
    tm = x_ref.shape[0]

    @pl.when(pl.program_id(0) == 0)
    def _():
        carry_ref[...] = jnp.zeros_like(carry_ref)

    h = x_ref[...] + jnp.dot(m_ref[...], wo_ref[...], preferred_element_type=F32)
    h_ref[...] = h
    n2 = _rms_norm(h, g_ref[...])
    n2_ref[...] = n2

    n_hi = n2.astype(BF16)
    n_lo = (n2 - n_hi.astype(F32)).astype(BF16)
    nt = (((1,), (1,)), ((), ()))
    lg = (lax.dot_general(wrh_ref[...], n_hi, nt, preferred_element_type=F32)
          + lax.dot_general(wrh_ref[...], n_lo, nt, preferred_element_type=F32)
          + lax.dot_general(wrl_ref[...], n_hi, nt, preferred_element_type=F32)
          + br_ref[...])

    eidx = lax.broadcasted_iota(jnp.int32, (experts, tm), 0)
    vals, onehots = [], []
    for k in range(TOP_K):
        mx = jnp.max(lg, axis=0, keepdims=True)
        ik = jnp.min(jnp.where(lg == mx, eidx, experts), axis=0, keepdims=True)
        sel = eidx == ik
        vals.append(mx)
        onehots.append(sel)
        idx_ref[k:k + 1, :] = ik
        lg = jnp.where(sel, -jnp.inf, lg)

    exps = [jnp.exp(v - vals[0]) for v in vals]
    denom = exps[0] + exps[1] + exps[2] + exps[3]
    for k in range(TOP_K):
        prob_ref[k:k + 1, :] = exps[k] / denom

    chosen = jnp.zeros((experts, tm), F32)
    for sel in onehots:
        chosen = chosen + sel.astype(F32)
    src = lax.broadcasted_iota(jnp.int32, (tm, tm), 0)
    dst = lax.broadcasted_iota(jnp.int32, (tm, tm), 1)
    before = (src < dst).astype(BF16)
    prior = jnp.dot(chosen.astype(BF16), before, preferred_element_type=F32) + carry_ref[...]
    for k in range(TOP_K):
        rk = jnp.sum(jnp.where(onehots[k], prior, 0.0), axis=0, keepdims=True)
        rank_ref[k:k + 1, :] = rk.astype(jnp.int32)
    carry_ref[...] = carry_ref[...] + jnp.sum(chosen, axis=1, keepdims=True)
    cnt_ref[...] = jnp.broadcast_to(carry_ref[...], cnt_ref.shape).astype(jnp.int32)


def _route_table_kernel(cnt_ref, idx_ref, rank_ref, dest_ref, meta_ref, *, experts, tile_log2):
    lanes = cnt_ref.shape[1]
    cnt = cnt_ref[...]
    padded = ((cnt + ((1 << tile_log2) - 1)) >> tile_log2) << tile_log2
    padded_f = padded.astype(F32)
    row = lax.broadcasted_iota(jnp.int32, (experts, lanes), 0)
    lane = lax.broadcasted_iota(jnp.int32, (experts, lanes), 1)
    padded_row = jnp.sum(jnp.where(row == lane, padded_f, 0.0), axis=0, keepdims=True)
    pstart_col = jnp.sum(jnp.where(lane < row, padded_row, 0.0), axis=1, keepdims=True)
    pend_col = pstart_col + padded_f[:, 0:1]

    tt = idx_ref.shape[1]
    eidx = lax.broadcasted_iota(jnp.int32, (experts, tt), 0)
    for k in range(TOP_K):
        sel = eidx == idx_ref[k:k + 1, :]
        start = jnp.sum(jnp.where(sel, pstart_col, 0.0), axis=0, keepdims=True)
        dest_ref[k:k + 1, :] = rank_ref[k:k + 1, :] + start.astype(jnp.int32)

    nb = meta_ref.shape[1]
    tile_start = (lax.broadcasted_iota(jnp.int32, (experts, nb), 1) << tile_log2).astype(F32)
    ecol = lax.broadcasted_iota(jnp.int32, (experts, nb), 0)
    be = jnp.minimum(jnp.sum((pend_col <= tile_start).astype(F32), axis=0, keepdims=True).astype(jnp.int32),
                     experts - 1)
    nonempty = padded_f[:, 0:1] > 0.0
    nxt = jnp.min(jnp.where(nonempty & (ecol > be), ecol, experts), axis=0, keepdims=True)
    first = jnp.min(jnp.where(nonempty, ecol, experts), axis=0, keepdims=True)
    total = jnp.sum(padded_f, axis=0, keepdims=True)[:, 0:1].astype(jnp.int32) >> tile_log2
    meta_ref[...] = jnp.zeros_like(meta_ref)
    meta_ref[0:1, :] = be
    meta_ref[1:2, :] = nxt
    meta_ref[2:3, :] = jnp.broadcast_to(total, (1, nb))
    meta_ref[3:4, :] = first
    pstart_row = jnp.sum(jnp.where(row < lane, padded_f, 0.0), axis=0, keepdims=True)
    cnt_row = jnp.sum(jnp.where(row == lane, cnt.astype(F32), 0.0), axis=0, keepdims=True)
    meta_ref[4:5, 0:lanes] = pstart_row.astype(jnp.int32)
    meta_ref[5:6, 0:lanes] = cnt_row.astype(jnp.int32)
    mine = ecol == be
    cnt_sel = jnp.sum(jnp.where(mine, cnt.astype(F32)[:, 0:1], 0.0), axis=0, keepdims=True)
    start_sel = jnp.sum(jnp.where(mine, pstart_col, 0.0), axis=0, keepdims=True)
    valid = jnp.clip(cnt_sel - (tile_start[0:1, :] - start_sel), 0.0, float(1 << tile_log2))
    meta_ref[6:7, :] = valid.astype(jnp.int32)


def _dispatch_kernel(ps_ref, cnt_ref, nu_ref, dest_ref, n2_ref, xp_ref, zero_ref, sem, zsem,
                     *, experts, n_blocks, tile_log2):
    tm = n2_ref.shape[0]
    te = 1 << tile_log2

    @pl.when(pl.program_id(0) == 0)
    def _():
        zero_ref[...] = jnp.zeros_like(zero_ref)

        def pad_rows(e):
            first = ps_ref[e] + cnt_ref[e]
            return first, ((cnt_ref[e] + (te - 1)) >> tile_log2 << tile_log2) - cnt_ref[e]

        def pad_copy(row):
            return pltpu.make_async_copy(zero_ref.at[pl.ds(0, 1)], xp_ref.at[pl.ds(row, 1)], zsem)

        def tile_copy(b):
            return pltpu.make_async_copy(zero_ref, xp_ref.at[pl.ds(pl.multiple_of(b * te, te), te)], zsem)

        def for_each_fill(fn_row, fn_tile):
            def per_expert(e, carry):
                first, n = pad_rows(e)
                lax.fori_loop(0, n, lambda r, c: (fn_row(first + r), c)[1], 0)
                return carry

            lax.fori_loop(0, experts, per_expert, 0)
            lax.fori_loop(nu_ref[0], n_blocks, lambda b, c: (fn_tile(b), c)[1], 0)

        for_each_fill(lambda row: pad_copy(row).start(), lambda b: tile_copy(b).start())
        for_each_fill(lambda row: pad_copy(row).wait(), lambda b: tile_copy(b).wait())

    def row_copy(t, d):
        return pltpu.make_async_copy(n2_ref.at[pl.ds(t, 1)], xp_ref.at[pl.ds(d, 1)], sem)

    def issue(t, carry):
        for k in range(TOP_K):
            row_copy(t, dest_ref[k, t]).start()
        return carry

    lax.fori_loop(0, tm, issue, 0)

    for k in range(TOP_K):
        pltpu.make_async_copy(n2_ref, xp_ref.at[pl.ds(0, tm)], sem).wait()


def _stream_expert_weights(be_ref, nx_ref, fe_ref, copies, load, *, experts, n_pass):
    p = pl.program_id(0)
    m = pl.program_id(1)
    e = be_ref[m]

    @pl.when((m == 0) | (e != be_ref[jnp.maximum(m - 1, 0)]))
    def _():
        @pl.when((m == 0) & (p == 0))
        def _():
            for c in copies(e, p):
                c.start()

        for c in copies(e, p):
            c.wait()
        load()
        same_pass = nx_ref[m] < experts
        e2 = jnp.where(same_pass, nx_ref[m], fe_ref[0])
        p2 = jnp.where(same_pass, p, p + 1)

        @pl.when(p2 < n_pass)
        def _():
            for c in copies(e2, p2):
                c.start(priority=1)


WEIGHT_CAST_ROWS = 128


def _cast_rows(dst_ref, src_ref):
    def body(i, carry):
        rows = pl.ds(pl.multiple_of(i * WEIGHT_CAST_ROWS, WEIGHT_CAST_ROWS), WEIGHT_CAST_ROWS)
        dst_ref[rows, :] = src_ref[rows, :].astype(BF16)
        return carry

    lax.fori_loop(0, dst_ref.shape[0] // WEIGHT_CAST_ROWS, body, 0)


EXPERT_SUB_ROWS = 256


def _for_each_sub_block(nv_ref, o_ref, compute):
    m = pl.program_id(1)
    for s in range(o_ref.shape[0] // EXPERT_SUB_ROWS):
        rows = slice(s * EXPERT_SUB_ROWS, (s + 1) * EXPERT_SUB_ROWS)
        if s == 0:
            compute(rows)
            continue
        pl.when(nv_ref[m] > s * EXPERT_SUB_ROWS)(functools.partial(compute, rows))

        @pl.when(nv_ref[m] <= s * EXPERT_SUB_ROWS)
        def _(rows=rows):
            o_ref[rows, :] = jnp.zeros((EXPERT_SUB_ROWS, o_ref.shape[1]), o_ref.dtype)


def _expert_up_kernel(be_ref, nx_ref, nu_ref, fe_ref, nv_ref, x_ref, bg_ref, bu_ref, w_hbm, o_ref,
                      stage_ref, wgb_ref, wub_ref, sem, *, experts, n_pass):
    m = pl.program_id(1)
    tn = o_ref.shape[1]
    de = n_pass * tn

    def copies(e, p):
        return [pltpu.make_async_copy(w_hbm.at[e, :, pl.ds(pl.multiple_of(half * de + p * tn, tn), tn)],
                                      stage_ref.at[half], sem.at[half]) for half in range(2)]

    def load():
        _cast_rows(wgb_ref, stage_ref.at[0])
        _cast_rows(wub_ref, stage_ref.at[1])

    @pl.when(m < nu_ref[0])
    def _():
        _stream_expert_weights(be_ref, nx_ref, fe_ref, copies, load, experts=experts, n_pass=n_pass)

        def compute(rows):
            x = x_ref[rows, :].astype(BF16)
            g = jnp.dot(x, wgb_ref[...], preferred_element_type=F32) + bg_ref[0]
            u = jnp.dot(x, wub_ref[...], preferred_element_type=F32) + bu_ref[0]
            g = jnp.minimum(g, SWIGLU_LIMIT)
            u = jnp.clip(u, -SWIGLU_LIMIT, SWIGLU_LIMIT)
            o_ref[rows, :] = ((u + 1.0) * (g * _sigmoid(SWIGLU_ALPHA * g))).astype(o_ref.dtype)

        _for_each_sub_block(nv_ref, o_ref, compute)

    @pl.when(m >= nu_ref[0])
    def _():
        o_ref[...] = jnp.zeros_like(o_ref)


def _expert_down_kernel(be_ref, nx_ref, nu_ref, fe_ref, nv_ref, h_ref, b_ref, w_hbm, o_ref, stage_ref, wb_ref,
                        sem, *, experts, n_pass):
    m = pl.program_id(1)
    tn = o_ref.shape[1]

    def copies(e, p):
        return [pltpu.make_async_copy(w_hbm.at[e, :, pl.ds(pl.multiple_of(p * tn, tn), tn)], stage_ref, sem)]

    def load():
        _cast_rows(wb_ref, stage_ref)

    @pl.when(m < nu_ref[0])
    def _():
        _stream_expert_weights(be_ref, nx_ref, fe_ref, copies, load, experts=experts, n_pass=n_pass)

        def compute(rows):
            o_ref[rows, :] = jnp.dot(h_ref[rows, :], wb_ref[...], preferred_element_type=F32) + b_ref[0]

        _for_each_sub_block(nv_ref, o_ref, compute)

    @pl.when(m >= nu_ref[0])
    def _():
        o_ref[...] = jnp.zeros_like(o_ref)


def _combine_kernel(dest_ref, dnext_ref, p_ref, h_ref, g_ref, yp_ref, o_ref, rows_ref, sem, *, n_tiles):
    i = pl.program_id(0)
    tm = h_ref.shape[0]
    slot = i % 2

    def row_copy(idx_ref, buf, t, k):
        return pltpu.make_async_copy(yp_ref.at[pl.ds(idx_ref[k, t], 1)], rows_ref.at[buf, k, pl.ds(t, 1)],
                                     sem.at[buf])

    def for_each_row(idx_ref, buf, fn):
        def body(t, carry):
            for k in range(TOP_K):
                fn(row_copy(idx_ref, buf, t, k), k)
            return carry

        lax.fori_loop(0, tm, body, 0)

    start = lambda c, k: c.start()

    @pl.when(i == 0)
    def _():
        for_each_row(dest_ref, 0, start)

    @pl.when(i + 1 < n_tiles)
    def _():
        for_each_row(dnext_ref, 1 - slot, start)

    for k in range(TOP_K):
        pltpu.make_async_copy(yp_ref.at[pl.ds(0, tm)], rows_ref.at[slot, k], sem.at[slot]).wait()

    y = h_ref[...]
    for k in range(TOP_K):
        y = y + p_ref[:, k:k + 1] * rows_ref[slot, k]
    o_ref[...] = _rms_norm(y, g_ref[...])


def kernel(x, mem, norm1_g, w_in, b_gate, w_dw, b_dw, conv_ln_g, conv_ln_b, w_conv_out, sgu_ln_g, sgu_ln_b,
           w_spatial, b_spatial, w_sgu_out, mem_norm_g, w_kv, w_xattn_out, w_out, norm2_g, w_router, b_router,
           w_gate_up, b_gate_up, w_down, b_down, final_norm_g):
    b, s, d = x.shape
    depth = norm1_g.shape[0]
    t = b * s
    nm = mem.shape[1]
    kw, cc = w_dw.shape[1], w_dw.shape[2]
    sw = w_sgu_out.shape[1]
    groups, chunk = w_spatial.shape[1], w_spatial.shape[2]
    xw = w_xattn_out.shape[1]
    heads = xw // (d // XATTN_HEAD_DIM_DIVISOR)
    experts = w_router.shape[2]
    de = w_down.shape[2]
    in_cols = w_in.shape[2]

    unit = math.gcd(math.gcd(cc, sw), math.gcd(xw, d))
    assert in_cols == 2 * cc + 2 * sw + xw + 3 * d and in_cols % unit == 0
    assert (sw // groups) % V7X_LANES == 0 and kw - 1 <= CONV_HALO and cc % CONV_LANES == 0
    assert (2 * cc) % sw == 0 and (2 * cc + 2 * sw) % xw == 0 and d % 2 == 0

    tm = min(512, s)
    assert s % tm == 0 and tm % chunk == 0 and tm % CONV_ROWS == 0
    tiles_per_seq = s // tm
    gate_col0 = (2 * cc + 2 * sw + xw) // unit

    h = x.reshape(t, d)
    for l in range(depth):
        proj, gates = _norm_matmul_split(h, norm1_g[l], w_in[l].astype(BF16), min(1024, t), unit, gate_col0)
        kv = _norm_matmul(mem.reshape(b * nm, d), mem_norm_g[l], w_kv[l].astype(BF16), min(512, b * nm), xw)
        bg = b_gate[l].reshape(1, 3 * d)

        def gate_specs(branch):
            return [pl.BlockSpec((tm, d), lambda i: (i, branch)), pl.BlockSpec((1, d), lambda i: (0, branch))]

        tile_spec = pl.BlockSpec((tm, d), lambda i: (i, 0))
        row_vec = lambda n: pl.BlockSpec((1, n), lambda i: (0, 0))
        whole = lambda shape: pl.BlockSpec(shape, lambda i: tuple(0 for _ in shape))

        wdw_b = jnp.broadcast_to(w_dw[l][:, None, :], (kw, V7X_SUBLANES, cc))
        gy = pl.pallas_call(
            functools.partial(_conv_kernel, tm=tm, tiles_per_seq=tiles_per_seq, kw=kw),
            grid=(t // tm,),
            in_specs=[pl.BlockSpec((tm, cc), lambda i: (i, 0)),
                      pl.BlockSpec((tm, cc), lambda i: (i, 1)),
                      *gate_specs(0),
                      whole((kw, V7X_SUBLANES, cc)),
                      row_vec(cc), row_vec(cc), row_vec(cc),
                      whole((cc, d))],
            out_specs=tile_spec,
            out_shape=jax.ShapeDtypeStruct((t, d), BF16),
            scratch_shapes=[pltpu.VMEM((V7X_SUBLANES, tm + CONV_HALO + V7X_SUBLANES, cc), F32),
                            pltpu.VMEM((CONV_HALO, cc), F32),
                            pltpu.VMEM((tm, cc), F32),
                            pltpu.VMEM((tm, cc), BF16)],
            compiler_params=_params("arbitrary"),
            name="conv_branch",
        )(proj, proj, gates, bg, wdw_b, b_dw[l].reshape(1, cc), conv_ln_g[l].reshape(1, cc),
          conv_ln_b[l].reshape(1, cc), w_conv_out[l].astype(BF16))

        bs_full = jnp.repeat(b_spatial[l].T, sw // groups, axis=1)
        gy = pl.pallas_call(
            functools.partial(_sgu_kernel, tm=tm, groups=groups, chunk=chunk),
            grid=(t // tm,),
            in_specs=[pl.BlockSpec((tm, sw), lambda i: (i, 2 * cc // sw)),
                      pl.BlockSpec((tm, sw), lambda i: (i, 2 * cc // sw + 1)),
                      *gate_specs(1),
                      tile_spec,
                      row_vec(sw), row_vec(sw),
                      whole((groups, chunk, chunk)),
                      whole((chunk, sw)),
                      whole((sw, d))],
            out_specs=tile_spec,
            out_shape=jax.ShapeDtypeStruct((t, d), BF16),
            scratch_shapes=[pltpu.VMEM((tm, sw), BF16)],
            compiler_params=_params("arbitrary"),
            name="sgu_branch",
        )(proj, proj, gates, bg, gy, sgu_ln_g[l].reshape(1, sw), sgu_ln_b[l].reshape(1, sw),
          w_spatial[l], bs_full, w_sgu_out[l].astype(BF16))

        merged = pl.pallas_call(
            functools.partial(_xattn_kernel, heads=heads),
            grid=(t // tm,),
            in_specs=[pl.BlockSpec((tm, xw), lambda i: (i, (2 * cc + 2 * sw) // xw)),
                      pl.BlockSpec((nm, xw), lambda i: (i // tiles_per_seq, 0)),
                      pl.BlockSpec((nm, xw), lambda i: (i // tiles_per_seq, 1)),
                      *gate_specs(2),
                      tile_spec,
                      whole((xw, d))],
            out_specs=tile_spec,
            out_shape=jax.ShapeDtypeStruct((t, d), BF16),
            scratch_shapes=[pltpu.VMEM((tm, xw), BF16)],
            compiler_params=_params("arbitrary"),
            name="xattn_branch",
        )(proj, kv, kv, gates, bg, gy, w_xattn_out[l].astype(BF16))

        wr = w_router[l].T
        wr_hi = wr.astype(BF16)
        wr_lo = (wr - wr_hi.astype(F32)).astype(BF16)
        tr = min(512, t)
        full = lambda shape: pl.BlockSpec(shape, lambda i: tuple(0 for _ in shape))
        kt_spec = pl.BlockSpec((TOP_K, tr), lambda i: (0, i))
        h, n2p, top_idx, probs, rank, counts = pl.pallas_call(
            functools.partial(_mix_route_kernel, experts=experts),
            grid=(t // tr,),
            in_specs=[pl.BlockSpec((tr, d), lambda i: (i, 0)),
                      pl.BlockSpec((tr, d), lambda i: (i, 0)),
                      full((d, d)), full((1, d)), full((experts, d)), full((experts, d)), full((experts, 1))],
            out_specs=[pl.BlockSpec((tr, d), lambda i: (i, 0)),
                       pl.BlockSpec((tr, d), lambda i: (i, 0)),
                       kt_spec, kt_spec, kt_spec,
                       full((experts, V7X_LANES))],
            out_shape=[jax.ShapeDtypeStruct((t, d), F32),
                       jax.ShapeDtypeStruct((t, d), F32),
                       jax.ShapeDtypeStruct((TOP_K, t), jnp.int32),
                       jax.ShapeDtypeStruct((TOP_K, t), F32),
                       jax.ShapeDtypeStruct((TOP_K, t), jnp.int32),
                       jax.ShapeDtypeStruct((experts, V7X_LANES), jnp.int32)],
            scratch_shapes=[pltpu.VMEM((experts, 1), F32)],
            compiler_params=_params("arbitrary"),
            name="mix_route",
        )(merged, h, w_out[l].astype(BF16), norm2_g[l].reshape(1, d), wr_hi, wr_lo, b_router[l].reshape(experts, 1))

        tile_log2 = 9
        te = 1 << tile_log2
        n_blocks = (t * TOP_K) // te + experts
        cap = n_blocks * te
        nb_pad = -(-n_blocks // V7X_LANES) * V7X_LANES
        tt = min(2048, t)
        dest, meta = pl.pallas_call(
            functools.partial(_route_table_kernel, experts=experts, tile_log2=tile_log2),
            grid=(t // tt,),
            in_specs=[pl.BlockSpec((experts, V7X_LANES), lambda i: (0, 0)),
                      pl.BlockSpec((TOP_K, tt), lambda i: (0, i)),
                      pl.BlockSpec((TOP_K, tt), lambda i: (0, i))],
            out_specs=[pl.BlockSpec((TOP_K, tt), lambda i: (0, i)),
                       pl.BlockSpec((V7X_SUBLANES, nb_pad), lambda i: (0, 0))],
            out_shape=[jax.ShapeDtypeStruct((TOP_K, t), jnp.int32),
                       jax.ShapeDtypeStruct((V7X_SUBLANES, nb_pad), jnp.int32)],
            compiler_params=_params("arbitrary"),
            name="route_table",
        )(counts, top_idx, rank)
        block_e, next_e = meta[0, :n_blocks], meta[1, :n_blocks]
        n_used, first_e, n_valid = meta[2, :1], meta[3, :1], meta[6, :n_blocks]

        td = min(256, t)
        xp = pl.pallas_call(
            functools.partial(_dispatch_kernel, experts=experts, n_blocks=n_blocks, tile_log2=tile_log2),
            grid_spec=pltpu.PrefetchScalarGridSpec(
                num_scalar_prefetch=3,
                grid=(t // td,),
                in_specs=[pl.BlockSpec((TOP_K, td), lambda i, ps, cn, nu: (0, i), memory_space=pltpu.SMEM),
                          pl.BlockSpec((td, d), lambda i, ps, cn, nu: (i, 0))],
                out_specs=pl.BlockSpec(memory_space=pl.ANY),
                scratch_shapes=[pltpu.VMEM((te, d), F32),
                                pltpu.SemaphoreType.DMA(()), pltpu.SemaphoreType.DMA(())]),
            out_shape=jax.ShapeDtypeStruct((cap, d), F32),
            compiler_params=_params("arbitrary"),
            name="dispatch",
        )(meta[4, :experts], meta[5, :experts], n_used, dest, n2p)

        tn_up = min(1024, de)
        n_up = de // tn_up
        clamp = lambda m, nu: jnp.minimum(m, nu[0] - 1)
        bias_up = b_gate_up[l].reshape(experts, 1, 2 * de)
        hidden = pl.pallas_call(
            functools.partial(_expert_up_kernel, experts=experts, n_pass=n_up),
            grid_spec=pltpu.PrefetchScalarGridSpec(
                num_scalar_prefetch=5,
                grid=(n_up, n_blocks),
                in_specs=[pl.BlockSpec((te, d), lambda p, m, be, nx, nu, fe, nv: (clamp(m, nu), 0)),
                          pl.BlockSpec((1, 1, tn_up), lambda p, m, be, nx, nu, fe, nv: (be[clamp(m, nu)], 0, p)),
                          pl.BlockSpec((1, 1, tn_up),
                                       lambda p, m, be, nx, nu, fe, nv: (be[clamp(m, nu)], 0, n_up + p)),
                          pl.BlockSpec(memory_space=pl.ANY)],
                out_specs=pl.BlockSpec((te, tn_up), lambda p, m, be, nx, nu, fe, nv: (m, p)),
                scratch_shapes=[pltpu.VMEM((2, d, tn_up), F32),
                                pltpu.VMEM((d, tn_up), BF16), pltpu.VMEM((d, tn_up), BF16),
                                pltpu.SemaphoreType.DMA((2,))]),
            out_shape=jax.ShapeDtypeStruct((cap, de), BF16),
            compiler_params=_params("arbitrary", "arbitrary"),
            name="expert_up",
        )(block_e, next_e, n_used, first_e, n_valid, xp, bias_up, bias_up, w_gate_up[l])

        tn_dn = min(2048, d)
        n_dn = d // tn_dn
        yp = pl.pallas_call(
            functools.partial(_expert_down_kernel, experts=experts, n_pass=n_dn),
            grid_spec=pltpu.PrefetchScalarGridSpec(
                num_scalar_prefetch=5,
                grid=(n_dn, n_blocks),
                in_specs=[pl.BlockSpec((te, de), lambda p, m, be, nx, nu, fe, nv: (clamp(m, nu), 0)),
                          pl.BlockSpec((1, 1, tn_dn), lambda p, m, be, nx, nu, fe, nv: (be[clamp(m, nu)], 0, p)),
                          pl.BlockSpec(memory_space=pl.ANY)],
                out_specs=pl.BlockSpec((te, tn_dn), lambda p, m, be, nx, nu, fe, nv: (m, p)),
                scratch_shapes=[pltpu.VMEM((de, tn_dn), F32), pltpu.VMEM((de, tn_dn), BF16),
                                pltpu.SemaphoreType.DMA(())]),
            out_shape=jax.ShapeDtypeStruct((cap, d), F32),
            compiler_params=_params("arbitrary", "arbitrary"),
            name="expert_down",
        )(block_e, next_e, n_used, first_e, n_valid, hidden, b_down[l].reshape(experts, 1, d), w_down[l])

        assert depth == 1
        tc = min(256, t)
        n_tc = t // tc
        h = pl.pallas_call(
            functools.partial(_combine_kernel, n_tiles=n_tc),
            grid=(n_tc,),
            in_specs=[pl.BlockSpec((TOP_K, tc), lambda i: (0, i), memory_space=pltpu.SMEM),
                      pl.BlockSpec((TOP_K, tc), lambda i: (0, jnp.minimum(i + 1, n_tc - 1)),
                                   memory_space=pltpu.SMEM),
                      pl.BlockSpec((tc, TOP_K), lambda i: (i, 0)),
                      pl.BlockSpec((tc, d), lambda i: (i, 0)),
                      pl.BlockSpec((1, d), lambda i: (0, 0)),
                      pl.BlockSpec(memory_space=pl.ANY)],
            out_specs=pl.BlockSpec((tc, d), lambda i: (i, 0)),
            out_shape=jax.ShapeDtypeStruct((t, d), F32),
            scratch_shapes=[pltpu.VMEM((2, TOP_K, tc, d), F32), pltpu.SemaphoreType.DMA((2,))],
            compiler_params=_params("arbitrary"),
            name="combine",
        )(dest, dest, probs.T, h, final_norm_g.reshape(1, d), yp)

    return h.reshape(b, s, d)
```

```python
import functools
import math

import jax
import jax.numpy as jnp
from jax import lax
from jax.experimental import pallas as pl
from jax.experimental.pallas import tpu as pltpu

EPS = 1e-5
TOP_K = 4
SWIGLU_LIMIT = 7.0
SWIGLU_ALPHA = 1.702
XATTN_HEAD_DIM_DIVISOR = 8

V7X_LANES = 128
V7X_SUBLANES = 8
V7X_VMEM_LIMIT_BYTES = 56 * 1024 * 1024

MIXER_ROWS = 512
INPROJ_ROWS = 1024
KV_ROWS = 512
ROUTE_TABLE_COLS = 2048
EXPERT_TILE_LOG2 = 8
EXPERT_UP_COLS = 1024
EXPERT_DOWN_COLS = 2048
PERMUTE_ROWS = 256

F32 = jnp.float32
BF16 = jnp.bfloat16


def _params(*sem):
    return pltpu.CompilerParams(dimension_semantics=sem, vmem_limit_bytes=V7X_VMEM_LIMIT_BYTES)


def _sigmoid(x):
    return 1.0 / (1.0 + jnp.exp(-x))


def _layer_norm(x, g, b):
    mu = jnp.mean(x, axis=-1, keepdims=True)
    xc = x - mu
    var = jnp.mean(xc * xc, axis=-1, keepdims=True)
    return xc * lax.rsqrt(var + EPS) * g + b


def _rms_norm(x, g):
    return x * lax.rsqrt(jnp.mean(x * x, axis=-1, keepdims=True) + EPS) * g


def _norm_matmul_kernel(x_ref, g_ref, w_ref, o_ref, xn_ref):
    @pl.when(pl.program_id(1) == 0)
    def _():
        xn_ref[...] = _rms_norm(x_ref[...], g_ref[...]).astype(BF16)

    o_ref[...] = jnp.dot(xn_ref[...], w_ref[...], preferred_element_type=F32).astype(o_ref.dtype)


def _norm_matmul(x2d, g, w_bf16, tm, tn):
    m, d = x2d.shape
    n = w_bf16.shape[1]
    return pl.pallas_call(
        _norm_matmul_kernel,
        grid=(m // tm, n // tn),
        in_specs=[
            pl.BlockSpec((tm, d), lambda i, j: (i, 0)),
            pl.BlockSpec((1, d), lambda i, j: (0, 0)),
            pl.BlockSpec((d, tn), lambda i, j: (0, j)),
        ],
        out_specs=pl.BlockSpec((tm, tn), lambda i, j: (i, j)),
        out_shape=jax.ShapeDtypeStruct((m, n), BF16),
        scratch_shapes=[pltpu.VMEM((tm, d), BF16)],
        compiler_params=_params("arbitrary", "arbitrary"),
        name="norm_matmul",
    )(x2d, g.reshape(1, d), w_bf16)


def _norm_matmul_split_kernel(x_ref, g_ref, w_ref, oa_ref, ob_ref, xn_ref):
    @pl.when(pl.program_id(1) == 0)
    def _():
        xn_ref[...] = _rms_norm(x_ref[...], g_ref[...]).astype(BF16)

    res = jnp.dot(xn_ref[...], w_ref[...], preferred_element_type=F32).astype(oa_ref.dtype)
    oa_ref[...] = res
    ob_ref[...] = res


def _norm_matmul_split(x2d, g, w_bf16, tm, tn, n_first):
    m, d = x2d.shape
    n_tiles = w_bf16.shape[1] // tn
    n_rest = n_tiles - n_first
    return pl.pallas_call(
        _norm_matmul_split_kernel,
        grid=(m // tm, n_tiles),
        in_specs=[
            pl.BlockSpec((tm, d), lambda i, j: (i, 0)),
            pl.BlockSpec((1, d), lambda i, j: (0, 0)),
            pl.BlockSpec((d, tn), lambda i, j: (0, j)),
        ],
        out_specs=[pl.BlockSpec((tm, tn), lambda i, j: (i, jnp.minimum(j, n_first))),
                   pl.BlockSpec((tm, tn), lambda i, j: (i, jnp.where(j >= n_first, j - n_first, n_rest)))],
        out_shape=[jax.ShapeDtypeStruct((m, (n_first + 1) * tn), BF16),
                   jax.ShapeDtypeStruct((m, (n_rest + 1) * tn), BF16)],
        scratch_shapes=[pltpu.VMEM((tm, d), BF16)],
        compiler_params=_params("arbitrary", "arbitrary"),
        name="norm_matmul_split",
    )(x2d, g.reshape(1, d), w_bf16)


CONV_HALO = 32
CONV_ROWS = 64
CONV_LANES = 256


def _conv_kernel(p_ref, q_ref, gl_ref, bg_ref, wdw_ref, bdw_ref, lg_ref, lb_ref, wo_ref, o_ref,
                 cp_ref, halo_ref, conv_ref, act_ref, *, tm, tiles_per_seq, kw):
    cc = p_ref.shape[1]
    back = (kw - 1) // V7X_SUBLANES * V7X_SUBLANES

    def branch():
        @pl.when(pl.program_id(0) % tiles_per_seq == 0)
        def _():
            halo_ref[...] = jnp.zeros_like(halo_ref)

        hg = p_ref[...].astype(F32) * _sigmoid(q_ref[...].astype(F32))
        halo = halo_ref[...]
        for r in range(V7X_SUBLANES):
            cp_ref[r, pl.ds(r, CONV_HALO), :] = halo
            cp_ref[r, pl.ds(CONV_HALO + r, tm), :] = hg
        halo_ref[...] = hg[tm - CONV_HALO:, :]

        for lc in range(cc // CONV_LANES):
            lanes = slice(lc * CONV_LANES, (lc + 1) * CONV_LANES)
            bias = bdw_ref[:, lanes]

            def body(rc, carry, lanes=lanes, bias=bias):
                base = rc * CONV_ROWS
                acc = jnp.zeros((CONV_ROWS // V7X_SUBLANES, V7X_SUBLANES, CONV_LANES), F32)
                for r in range(V7X_SUBLANES):
                    start = pl.multiple_of(base + (CONV_HALO - back), V7X_SUBLANES)
                    big = cp_ref[r, pl.ds(start, CONV_ROWS + back), lanes]
                    for k in range(kw):
                        delay = kw - 1 - k
                        if delay % V7X_SUBLANES != r:
                            continue
                        off = back - (delay - r)
                        win = big[off:off + CONV_ROWS].reshape(CONV_ROWS // V7X_SUBLANES, V7X_SUBLANES, CONV_LANES)
                        acc = acc + win * wdw_ref[k, :, lanes][None]
                conv_ref[pl.ds(pl.multiple_of(base, CONV_ROWS), CONV_ROWS), lanes] = (
                    acc.reshape(CONV_ROWS, CONV_LANES) + bias)
                return carry

            lax.fori_loop(0, tm // CONV_ROWS, body, 0)

        y = _layer_norm(conv_ref[...], lg_ref[...], lb_ref[...])
        act_ref[...] = (y * _sigmoid(y)).astype(BF16)

    branch()
    yo = jnp.dot(act_ref[...], wo_ref[...], preferred_element_type=F32)
    gate = _sigmoid(gl_ref[...].astype(F32) + bg_ref[...])
    o_ref[...] = (gate * yo).astype(o_ref.dtype)


def _gelu(x):
    return 0.5 * x * (1.0 + lax.erf(x * (1.0 / math.sqrt(2.0))))


def _sgu_kernel(u_ref, v_ref, gl_ref, bg_ref, acc_ref, lg_ref, lb_ref, ws_ref, bs_ref, wo_ref, o_ref,
                s_ref, *, tm, groups, chunk):
    sw = u_ref.shape[1]
    gd = sw // groups
    nchunk = tm // chunk

    def branch():
        u = _gelu(u_ref[...].astype(F32))
        v = _gelu(v_ref[...].astype(F32))
        vb = _layer_norm(v, lg_ref[...], lb_ref[...]).astype(BF16)
        row = lax.broadcasted_iota(jnp.int32, (chunk, chunk), 0)
        col = lax.broadcasted_iota(jnp.int32, (chunk, chunk), 1)
        causal = row >= col
        for g in range(groups):
            lanes = slice(g * gd, (g + 1) * gd)
            wg = jnp.where(causal, ws_ref[g], 0.0).astype(BF16)
            vcat = jnp.concatenate([vb[c * chunk:(c + 1) * chunk, lanes] for c in range(nchunk)], axis=1)
            og = jnp.dot(wg, vcat, preferred_element_type=F32)
            for c in range(nchunk):
                rows = slice(c * chunk, (c + 1) * chunk)
                sv = og[:, c * gd:(c + 1) * gd] + bs_ref[:, lanes]
                s_ref[rows, lanes] = (u[rows, lanes] * sv).astype(BF16)

    branch()
    yo = jnp.dot(s_ref[...], wo_ref[...], preferred_element_type=F32)
    gate = _sigmoid(gl_ref[...].astype(F32) + bg_ref[...])
    o_ref[...] = (acc_ref[...].astype(F32) + gate * yo).astype(o_ref.dtype)


def _xattn_kernel(q_ref, k_ref, v_ref, gl_ref, bg_ref, acc_ref, wo_ref, o_ref, att_ref, *, heads):
    xw = q_ref.shape[1]
    hd = xw // heads

    def branch():
        scale = hd ** -0.5
        for h in range(heads):
            lanes = slice(h * hd, (h + 1) * hd)
            s = lax.dot_general(q_ref[:, lanes], k_ref[:, lanes], (((1,), (1,)), ((), ())),
                                preferred_element_type=F32) * scale
            s = s - jnp.max(s, axis=-1, keepdims=True)
            e = jnp.exp(s)
            p = e / jnp.sum(e, axis=-1, keepdims=True)
            oh = jnp.dot(p.astype(BF16), v_ref[:, lanes], preferred_element_type=F32)
            att_ref[:, lanes] = oh.astype(BF16)

    branch()
    yo = jnp.dot(att_ref[...], wo_ref[...], preferred_element_type=F32)
    gate = _sigmoid(gl_ref[...].astype(F32) + bg_ref[...])
    o_ref[...] = (acc_ref[...].astype(F32) + gate * yo).astype(o_ref.dtype)


def _mix_route_kernel(m_ref, x_ref, wo_ref, g_ref, wrh_ref, wrl_ref, br_ref,
                      h_ref, n2_ref, idx_ref, prob_ref, rank_ref, cnt_ref, carry_ref, *, experts):
    tm = x_ref.shape[0]

    @pl.when(pl.program_id(0) == 0)
    def _():
        carry_ref[...] = jnp.zeros_like(carry_ref)

    h = x_ref[...] + jnp.dot(m_ref[...], wo_ref[...], preferred_element_type=F32)
    h_ref[...] = h
    n2 = _rms_norm(h, g_ref[...])
    n2_ref[...] = n2

    n_hi = n2.astype(BF16)
    n_lo = (n2 - n_hi.astype(F32)).astype(BF16)
    nt = (((1,), (1,)), ((), ()))
    lg = (lax.dot_general(wrh_ref[...], n_hi, nt, preferred_element_type=F32)
          + lax.dot_general(wrh_ref[...], n_lo, nt, preferred_element_type=F32)
          + lax.dot_general(wrl_ref[...], n_hi, nt, preferred_element_type=F32)
          + br_ref[...])

    eidx = lax.broadcasted_iota(jnp.int32, (experts, tm), 0)
    vals, onehots = [], []
    for k in range(TOP_K):
        mx = jnp.max(lg, axis=0, keepdims=True)
        ik = jnp.min(jnp.where(lg == mx, eidx, experts), axis=0, keepdims=True)
        sel = eidx == ik
        vals.append(mx)
        onehots.append(sel)
        idx_ref[k:k + 1, :] = ik
        lg = jnp.where(sel, -jnp.inf, lg)

    exps = [jnp.exp(v - vals[0]) for v in vals]
    denom = exps[0] + exps[1] + exps[2] + exps[3]
    for k in range(TOP_K):
        prob_ref[k:k + 1, :] = exps[k] / denom

    chosen = jnp.zeros((experts, tm), F32)
    for sel in onehots:
        chosen = chosen + sel.astype(F32)
    src = lax.broadcasted_iota(jnp.int32, (tm, tm), 0)
    dst = lax.broadcasted_iota(jnp.int32, (tm, tm), 1)
    before = (src < dst).astype(BF16)
    prior = jnp.dot(chosen.astype(BF16), before, preferred_element_type=F32) + carry_ref[...]
    for k in range(TOP_K):
        rk = jnp.sum(jnp.where(onehots[k], prior, 0.0), axis=0, keepdims=True)
        rank_ref[k:k + 1, :] = rk.astype(jnp.int32)
    carry_ref[...] = carry_ref[...] + jnp.sum(chosen, axis=1, keepdims=True)
    cnt_ref[...] = jnp.broadcast_to(carry_ref[...], cnt_ref.shape).astype(jnp.int32)


def _route_table_kernel(cnt_ref, idx_ref, rank_ref, dest_ref, meta_ref, *, experts, tile_log2):
    lanes = cnt_ref.shape[1]
    cnt = cnt_ref[...]
    padded = ((cnt + ((1 << tile_log2) - 1)) >> tile_log2) << tile_log2
    padded_f = padded.astype(F32)
    row = lax.broadcasted_iota(jnp.int32, (experts, lanes), 0)
    lane = lax.broadcasted_iota(jnp.int32, (experts, lanes), 1)
    padded_row = jnp.sum(jnp.where(row == lane, padded_f, 0.0), axis=0, keepdims=True)
    pstart_col = jnp.sum(jnp.where(lane < row, padded_row, 0.0), axis=1, keepdims=True)
    pend_col = pstart_col + padded_f[:, 0:1]

    tt = idx_ref.shape[1]
    eidx = lax.broadcasted_iota(jnp.int32, (experts, tt), 0)
    for k in range(TOP_K):
        sel = eidx == idx_ref[k:k + 1, :]
        start = jnp.sum(jnp.where(sel, pstart_col, 0.0), axis=0, keepdims=True)
        dest_ref[k:k + 1, :] = rank_ref[k:k + 1, :] + start.astype(jnp.int32)

    nb = meta_ref.shape[1]
    tile_start = (lax.broadcasted_iota(jnp.int32, (experts, nb), 1) << tile_log2).astype(F32)
    ecol = lax.broadcasted_iota(jnp.int32, (experts, nb), 0)
    be = jnp.minimum(jnp.sum((pend_col <= tile_start).astype(F32), axis=0, keepdims=True).astype(jnp.int32),
                     experts - 1)
    nonempty = padded_f[:, 0:1] > 0.0
    nxt = jnp.min(jnp.where(nonempty & (ecol > be), ecol, experts), axis=0, keepdims=True)
    first = jnp.min(jnp.where(nonempty, ecol, experts), axis=0, keepdims=True)
    total = jnp.sum(padded_f, axis=0, keepdims=True)[:, 0:1].astype(jnp.int32) >> tile_log2
    meta_ref[...] = jnp.zeros_like(meta_ref)
    meta_ref[0:1, :] = be
    meta_ref[1:2, :] = nxt
    meta_ref[2:3, :] = jnp.broadcast_to(total, (1, nb))
    meta_ref[3:4, :] = first
    pstart_row = jnp.sum(jnp.where(row < lane, padded_f, 0.0), axis=0, keepdims=True)
    cnt_row = jnp.sum(jnp.where(row == lane, cnt.astype(F32), 0.0), axis=0, keepdims=True)
    meta_ref[4:5, 0:lanes] = pstart_row.astype(jnp.int32)
    meta_ref[5:6, 0:lanes] = cnt_row.astype(jnp.int32)


def _dispatch_kernel(ps_ref, cnt_ref, nu_ref, dest_ref, n2_ref, xp_ref, zero_ref, sem, zsem,
                     *, experts, n_blocks, tile_log2, n_assign):
    tm = n2_ref.shape[0]
    te = 1 << tile_log2

    def pad_copy(row):
        return pltpu.make_async_copy(zero_ref.at[pl.ds(0, 1)], xp_ref.at[pl.ds(row, 1)], zsem)

    def tile_copy(b):
        return pltpu.make_async_copy(zero_ref, xp_ref.at[pl.ds(pl.multiple_of(b * te, te), te)], zsem)

    @pl.when(pl.program_id(0) == 0)
    def _():
        zero_ref[...] = jnp.zeros_like(zero_ref)

        def per_expert(e, carry):
            first = ps_ref[e] + cnt_ref[e]
            n = ((cnt_ref[e] + (te - 1)) >> tile_log2 << tile_log2) - cnt_ref[e]
            lax.fori_loop(0, n, lambda r, c: (pad_copy(first + r).start(), c)[1], 0)
            return carry

        lax.fori_loop(0, experts, per_expert, 0)
        lax.fori_loop(nu_ref[0], n_blocks, lambda b, c: (tile_copy(b).start(), c)[1], 0)

    def row_copy(t, d):
        return pltpu.make_async_copy(n2_ref.at[pl.ds(t, 1)], xp_ref.at[pl.ds(d, 1)], sem)

    def issue(t, carry):
        for k in range(TOP_K):
            row_copy(t, dest_ref[k, t]).start()
        return carry

    lax.fori_loop(0, tm, issue, 0)

    for k in range(TOP_K):
        pltpu.make_async_copy(n2_ref, xp_ref.at[pl.ds(0, tm)], sem).wait()

    @pl.when(pl.program_id(0) == pl.num_programs(0) - 1)
    def _():
        pad_total = (nu_ref[0] << tile_log2) - n_assign
        n_tile_waits = (pad_total >> tile_log2) + (n_blocks - nu_ref[0])
        lax.fori_loop(0, n_tile_waits, lambda b, c: (tile_copy(0).wait(), c)[1], 0)
        lax.fori_loop(0, pad_total & (te - 1), lambda r, c: (pad_copy(0).wait(), c)[1], 0)


def _stream_expert_weights(be_ref, nx_ref, fe_ref, copies, load, *, experts, n_pass):
    p = pl.program_id(0)
    m = pl.program_id(1)
    e = be_ref[m]

    @pl.when((m == 0) | (e != be_ref[jnp.maximum(m - 1, 0)]))
    def _():
        @pl.when((m == 0) & (p == 0))
        def _():
            for c in copies(e, p):
                c.start()

        for c in copies(e, p):
            c.wait()
        load()
        same_pass = nx_ref[m] < experts
        e2 = jnp.where(same_pass, nx_ref[m], fe_ref[0])
        p2 = jnp.where(same_pass, p, p + 1)

        @pl.when(p2 < n_pass)
        def _():
            for c in copies(e2, p2):
                c.start(priority=1)


WEIGHT_CAST_ROWS = 128


def _cast_rows(dst_ref, src_ref):
    def body(i, carry):
        rows = pl.ds(pl.multiple_of(i * WEIGHT_CAST_ROWS, WEIGHT_CAST_ROWS), WEIGHT_CAST_ROWS)
        dst_ref[rows, :] = src_ref[rows, :].astype(BF16)
        return carry

    lax.fori_loop(0, dst_ref.shape[0] // WEIGHT_CAST_ROWS, body, 0)


def _expert_up_kernel(be_ref, nx_ref, nu_ref, fe_ref, x_ref, bg_ref, bu_ref, w_hbm, o_ref,
                      stage_ref, wgb_ref, wub_ref, sem, *, experts, n_pass):
    m = pl.program_id(1)
    tn = o_ref.shape[1]
    de = n_pass * tn

    def copies(e, p):
        return [pltpu.make_async_copy(w_hbm.at[e, :, pl.ds(pl.multiple_of(half * de + p * tn, tn), tn)],
                                      stage_ref.at[half], sem.at[half]) for half in range(2)]

    def load():
        _cast_rows(wgb_ref, stage_ref.at[0])
        _cast_rows(wub_ref, stage_ref.at[1])

    @pl.when(m < nu_ref[0])
    def _():
        _stream_expert_weights(be_ref, nx_ref, fe_ref, copies, load, experts=experts, n_pass=n_pass)
        x = x_ref[...].astype(BF16)
        g = jnp.dot(x, wgb_ref[...], preferred_element_type=F32) + bg_ref[0]
        u = jnp.dot(x, wub_ref[...], preferred_element_type=F32) + bu_ref[0]
        g = jnp.minimum(g, SWIGLU_LIMIT)
        u = jnp.clip(u, -SWIGLU_LIMIT, SWIGLU_LIMIT)
        o_ref[...] = ((u + 1.0) * (g * _sigmoid(SWIGLU_ALPHA * g))).astype(o_ref.dtype)

    @pl.when(m >= nu_ref[0])
    def _():
        o_ref[...] = jnp.zeros_like(o_ref)


def _expert_down_kernel(be_ref, nx_ref, nu_ref, fe_ref, h_ref, b_ref, w_hbm, o_ref, stage_ref, wb_ref, sem,
                        *, experts, n_pass):
    m = pl.program_id(1)
    tn = o_ref.shape[1]

    def copies(e, p):
        return [pltpu.make_async_copy(w_hbm.at[e, :, pl.ds(pl.multiple_of(p * tn, tn), tn)], stage_ref, sem)]

    def load():
        _cast_rows(wb_ref, stage_ref)

    @pl.when(m < nu_ref[0])
    def _():
        _stream_expert_weights(be_ref, nx_ref, fe_ref, copies, load, experts=experts, n_pass=n_pass)
        o_ref[...] = jnp.dot(h_ref[...], wb_ref[...], preferred_element_type=F32) + b_ref[0]

    @pl.when(m >= nu_ref[0])
    def _():
        o_ref[...] = jnp.zeros_like(o_ref)


def _combine_kernel(dest_ref, dnext_ref, p_ref, h_ref, g_ref, yp_ref, o_ref, rows_ref, sem, *, n_tiles):
    i = pl.program_id(0)
    tm = h_ref.shape[0]
    slot = i % 2

    def row_copy(idx_ref, buf, t, k):
        return pltpu.make_async_copy(yp_ref.at[pl.ds(idx_ref[k, t], 1)], rows_ref.at[buf, k, pl.ds(t, 1)],
                                     sem.at[buf])

    def for_each_row(idx_ref, buf, fn):
        def body(t, carry):
            for k in range(TOP_K):
                fn(row_copy(idx_ref, buf, t, k), k)
            return carry

        lax.fori_loop(0, tm, body, 0)

    start = lambda c, k: c.start()

    @pl.when(i == 0)
    def _():
        for_each_row(dest_ref, 0, start)

    @pl.when(i + 1 < n_tiles)
    def _():
        for_each_row(dnext_ref, 1 - slot, start)

    for k in range(TOP_K):
        pltpu.make_async_copy(yp_ref.at[pl.ds(0, tm)], rows_ref.at[slot, k], sem.at[slot]).wait()

    y = h_ref[...]
    for k in range(TOP_K):
        y = y + p_ref[:, k:k + 1] * rows_ref[slot, k]
    o_ref[...] = _rms_norm(y, g_ref[...])


def kernel(x, mem, norm1_g, w_in, b_gate, w_dw, b_dw, conv_ln_g, conv_ln_b, w_conv_out, sgu_ln_g, sgu_ln_b,
           w_spatial, b_spatial, w_sgu_out, mem_norm_g, w_kv, w_xattn_out, w_out, norm2_g, w_router, b_router,
           w_gate_up, b_gate_up, w_down, b_down, final_norm_g):
    b, s, d = x.shape
    depth = norm1_g.shape[0]
    t = b * s
    nm = mem.shape[1]
    kw, cc = w_dw.shape[1], w_dw.shape[2]
    sw = w_sgu_out.shape[1]
    groups, chunk = w_spatial.shape[1], w_spatial.shape[2]
    xw = w_xattn_out.shape[1]
    heads = xw // (d // XATTN_HEAD_DIM_DIVISOR)
    experts = w_router.shape[2]
    de = w_down.shape[2]
    in_cols = w_in.shape[2]

    unit = math.gcd(math.gcd(cc, sw), math.gcd(xw, d))
    assert in_cols == 2 * cc + 2 * sw + xw + 3 * d and in_cols % unit == 0
    assert (sw // groups) % V7X_LANES == 0 and kw - 1 <= CONV_HALO and cc % CONV_LANES == 0
    assert (2 * cc) % sw == 0 and (2 * cc + 2 * sw) % xw == 0 and d % 2 == 0

    tm = min(MIXER_ROWS, s)
    assert s % tm == 0 and tm % chunk == 0 and tm % CONV_ROWS == 0
    tiles_per_seq = s // tm
    gate_col0 = (2 * cc + 2 * sw + xw) // unit

    h = x.reshape(t, d)
    for l in range(depth):
        proj, gates = _norm_matmul_split(h, norm1_g[l], w_in[l].astype(BF16), min(INPROJ_ROWS, t), unit, gate_col0)
        kv = _norm_matmul(mem.reshape(b * nm, d), mem_norm_g[l], w_kv[l].astype(BF16), min(KV_ROWS, b * nm), xw)
        bg = b_gate[l].reshape(1, 3 * d)

        def gate_specs(branch):
            return [pl.BlockSpec((tm, d), lambda i: (i, branch)), pl.BlockSpec((1, d), lambda i: (0, branch))]

        tile_spec = pl.BlockSpec((tm, d), lambda i: (i, 0))
        row_vec = lambda n: pl.BlockSpec((1, n), lambda i: (0, 0))
        whole = lambda shape: pl.BlockSpec(shape, lambda i: tuple(0 for _ in shape))

        wdw_b = jnp.broadcast_to(w_dw[l][:, None, :], (kw, V7X_SUBLANES, cc))
        gy = pl.pallas_call(
            functools.partial(_conv_kernel, tm=tm, tiles_per_seq=tiles_per_seq, kw=kw),
            grid=(t // tm,),
            in_specs=[pl.BlockSpec((tm, cc), lambda i: (i, 0)),
                      pl.BlockSpec((tm, cc), lambda i: (i, 1)),
                      *gate_specs(0),
                      whole((kw, V7X_SUBLANES, cc)),
                      row_vec(cc), row_vec(cc), row_vec(cc),
                      whole((cc, d))],
            out_specs=tile_spec,
            out_shape=jax.ShapeDtypeStruct((t, d), BF16),
            scratch_shapes=[pltpu.VMEM((V7X_SUBLANES, tm + CONV_HALO + V7X_SUBLANES, cc), F32),
                            pltpu.VMEM((CONV_HALO, cc), F32),
                            pltpu.VMEM((tm, cc), F32),
                            pltpu.VMEM((tm, cc), BF16)],
            compiler_params=_params("arbitrary"),
            name="conv_branch",
        )(proj, proj, gates, bg, wdw_b, b_dw[l].reshape(1, cc), conv_ln_g[l].reshape(1, cc),
          conv_ln_b[l].reshape(1, cc), w_conv_out[l].astype(BF16))

        bs_full = jnp.repeat(b_spatial[l].T, sw // groups, axis=1)
        gy = pl.pallas_call(
            functools.partial(_sgu_kernel, tm=tm, groups=groups, chunk=chunk),
            grid=(t // tm,),
            in_specs=[pl.BlockSpec((tm, sw), lambda i: (i, 2 * cc // sw)),
                      pl.BlockSpec((tm, sw), lambda i: (i, 2 * cc // sw + 1)),
                      *gate_specs(1),
                      tile_spec,
                      row_vec(sw), row_vec(sw),
                      whole((groups, chunk, chunk)),
                      whole((chunk, sw)),
                      whole((sw, d))],
            out_specs=tile_spec,
            out_shape=jax.ShapeDtypeStruct((t, d), BF16),
            scratch_shapes=[pltpu.VMEM((tm, sw), BF16)],
            compiler_params=_params("arbitrary"),
            name="sgu_branch",
        )(proj, proj, gates, bg, gy, sgu_ln_g[l].reshape(1, sw), sgu_ln_b[l].reshape(1, sw),
          w_spatial[l], bs_full, w_sgu_out[l].astype(BF16))

        merged = pl.pallas_call(
            functools.partial(_xattn_kernel, heads=heads),
            grid=(t // tm,),
            in_specs=[pl.BlockSpec((tm, xw), lambda i: (i, (2 * cc + 2 * sw) // xw)),
                      pl.BlockSpec((nm, xw), lambda i: (i // tiles_per_seq, 0)),
                      pl.BlockSpec((nm, xw), lambda i: (i // tiles_per_seq, 1)),
                      *gate_specs(2),
                      tile_spec,
                      whole((xw, d))],
            out_specs=tile_spec,
            out_shape=jax.ShapeDtypeStruct((t, d), BF16),
            scratch_shapes=[pltpu.VMEM((tm, xw), BF16)],
            compiler_params=_params("arbitrary"),
            name="xattn_branch",
        )(proj, kv, kv, gates, bg, gy, w_xattn_out[l].astype(BF16))

        wr = w_router[l].T
        wr_hi = wr.astype(BF16)
        wr_lo = (wr - wr_hi.astype(F32)).astype(BF16)
        tr = min(MIXER_ROWS, t)
        full = lambda shape: pl.BlockSpec(shape, lambda i: tuple(0 for _ in shape))
        kt_spec = pl.BlockSpec((TOP_K, tr), lambda i: (0, i))
        h, n2p, top_idx, probs, rank, counts = pl.pallas_call(
            functools.partial(_mix_route_kernel, experts=experts),
            grid=(t // tr,),
            in_specs=[pl.BlockSpec((tr, d), lambda i: (i, 0)),
                      pl.BlockSpec((tr, d), lambda i: (i, 0)),
                      full((d, d)), full((1, d)), full((experts, d)), full((experts, d)), full((experts, 1))],
            out_specs=[pl.BlockSpec((tr, d), lambda i: (i, 0)),
                       pl.BlockSpec((tr, d), lambda i: (i, 0)),
                       kt_spec, kt_spec, kt_spec,
                       full((experts, V7X_LANES))],
            out_shape=[jax.ShapeDtypeStruct((t, d), F32),
                       jax.ShapeDtypeStruct((t, d), F32),
                       jax.ShapeDtypeStruct((TOP_K, t), jnp.int32),
                       jax.ShapeDtypeStruct((TOP_K, t), F32),
                       jax.ShapeDtypeStruct((TOP_K, t), jnp.int32),
                       jax.ShapeDtypeStruct((experts, V7X_LANES), jnp.int32)],
            scratch_shapes=[pltpu.VMEM((experts, 1), F32)],
            compiler_params=_params("arbitrary"),
            name="mix_route",
        )(merged, h, w_out[l].astype(BF16), norm2_g[l].reshape(1, d), wr_hi, wr_lo, b_router[l].reshape(experts, 1))

        tile_log2 = EXPERT_TILE_LOG2
        te = 1 << tile_log2
        n_blocks = (t * TOP_K) // te + experts
        cap = n_blocks * te
        nb_pad = -(-n_blocks // V7X_LANES) * V7X_LANES
        tt = min(ROUTE_TABLE_COLS, t)
        dest, meta = pl.pallas_call(
            functools.partial(_route_table_kernel, experts=experts, tile_log2=tile_log2),
            grid=(t // tt,),
            in_specs=[pl.BlockSpec((experts, V7X_LANES), lambda i: (0, 0)),
                      pl.BlockSpec((TOP_K, tt), lambda i: (0, i)),
                      pl.BlockSpec((TOP_K, tt), lambda i: (0, i))],
            out_specs=[pl.BlockSpec((TOP_K, tt), lambda i: (0, i)),
                       pl.BlockSpec((V7X_SUBLANES, nb_pad), lambda i: (0, 0))],
            out_shape=[jax.ShapeDtypeStruct((TOP_K, t), jnp.int32),
                       jax.ShapeDtypeStruct((V7X_SUBLANES, nb_pad), jnp.int32)],
            compiler_params=_params("arbitrary"),
            name="route_table",
        )(counts, top_idx, rank)
        block_e, next_e = meta[0, :n_blocks], meta[1, :n_blocks]
        n_used, first_e = meta[2, :1], meta[3, :1]

        td = min(PERMUTE_ROWS, t)
        xp = pl.pallas_call(
            functools.partial(_dispatch_kernel, experts=experts, n_blocks=n_blocks, tile_log2=tile_log2,
                              n_assign=t * TOP_K),
            grid_spec=pltpu.PrefetchScalarGridSpec(
                num_scalar_prefetch=3,
                grid=(t // td,),
                in_specs=[pl.BlockSpec((TOP_K, td), lambda i, ps, cn, nu: (0, i), memory_space=pltpu.SMEM),
                          pl.BlockSpec((td, d), lambda i, ps, cn, nu: (i, 0))],
                out_specs=pl.BlockSpec(memory_space=pl.ANY),
                scratch_shapes=[pltpu.VMEM((te, d), F32),
                                pltpu.SemaphoreType.DMA(()), pltpu.SemaphoreType.DMA(())]),
            out_shape=jax.ShapeDtypeStruct((cap, d), F32),
            compiler_params=_params("arbitrary"),
            name="dispatch",
        )(meta[4, :experts], meta[5, :experts], n_used, dest, n2p)

        tn_up = min(EXPERT_UP_COLS, de)
        n_up = de // tn_up
        clamp = lambda m, nu: jnp.minimum(m, nu[0] - 1)
        bias_up = b_gate_up[l].reshape(experts, 1, 2 * de)
        hidden = pl.pallas_call(
            functools.partial(_expert_up_kernel, experts=experts, n_pass=n_up),
            grid_spec=pltpu.PrefetchScalarGridSpec(
                num_scalar_prefetch=4,
                grid=(n_up, n_blocks),
                in_specs=[pl.BlockSpec((te, d), lambda p, m, be, nx, nu, fe: (clamp(m, nu), 0)),
                          pl.BlockSpec((1, 1, tn_up), lambda p, m, be, nx, nu, fe: (be[clamp(m, nu)], 0, p)),
                          pl.BlockSpec((1, 1, tn_up),
                                       lambda p, m, be, nx, nu, fe: (be[clamp(m, nu)], 0, n_up + p)),
                          pl.BlockSpec(memory_space=pl.ANY)],
                out_specs=pl.BlockSpec((te, tn_up), lambda p, m, be, nx, nu, fe: (m, p)),
                scratch_shapes=[pltpu.VMEM((2, d, tn_up), F32),
                                pltpu.VMEM((d, tn_up), BF16), pltpu.VMEM((d, tn_up), BF16),
                                pltpu.SemaphoreType.DMA((2,))]),
            out_shape=jax.ShapeDtypeStruct((cap, de), BF16),
            compiler_params=_params("arbitrary", "arbitrary"),
            name="expert_up",
        )(block_e, next_e, n_used, first_e, xp, bias_up, bias_up, w_gate_up[l])

        tn_dn = min(EXPERT_DOWN_COLS, d)
        n_dn = d // tn_dn
        yp = pl.pallas_call(
            functools.partial(_expert_down_kernel, experts=experts, n_pass=n_dn),
            grid_spec=pltpu.PrefetchScalarGridSpec(
                num_scalar_prefetch=4,
                grid=(n_dn, n_blocks),
                in_specs=[pl.BlockSpec((te, de), lambda p, m, be, nx, nu, fe: (clamp(m, nu), 0)),
                          pl.BlockSpec((1, 1, tn_dn), lambda p, m, be, nx, nu, fe: (be[clamp(m, nu)], 0, p)),
                          pl.BlockSpec(memory_space=pl.ANY)],
                out_specs=pl.BlockSpec((te, tn_dn), lambda p, m, be, nx, nu, fe: (m, p)),
                scratch_shapes=[pltpu.VMEM((de, tn_dn), F32), pltpu.VMEM((de, tn_dn), BF16),
                                pltpu.SemaphoreType.DMA(())]),
            out_shape=jax.ShapeDtypeStruct((cap, d), F32),
            compiler_params=_params("arbitrary", "arbitrary"),
            name="expert_down",
        )(block_e, next_e, n_used, first_e, hidden, b_down[l].reshape(experts, 1, d), w_down[l])

        assert depth == 1
        tc = min(PERMUTE_ROWS, t)
        n_tc = t // tc
        h = pl.pallas_call(
            functools.partial(_combine_kernel, n_tiles=n_tc),
            grid=(n_tc,),
            in_specs=[pl.BlockSpec((TOP_K, tc), lambda i: (0, i), memory_space=pltpu.SMEM),
                      pl.BlockSpec((TOP_K, tc), lambda i: (0, jnp.minimum(i + 1, n_tc - 1)),
                                   memory_space=pltpu.SMEM),
                      pl.BlockSpec((tc, TOP_K), lambda i: (i, 0)),
                      pl.BlockSpec((tc, d), lambda i: (i, 0)),
                      pl.BlockSpec((1, d), lambda i: (0, 0)),
                      pl.BlockSpec(memory_space=pl.ANY)],
            out_specs=pl.BlockSpec((tc, d), lambda i: (i, 0)),
            out_shape=jax.ShapeDtypeStruct((t, d), F32),
            scratch_shapes=[pltpu.VMEM((2, TOP_K, tc, d), F32), pltpu.SemaphoreType.DMA((2,))],
            compiler_params=_params("arbitrary"),
            name="combine",
        )(dest, dest, probs.T, h, final_norm_g.reshape(1, d), yp)

    return h.reshape(b, s, d)
```

```python
import functools
import math

import jax
import jax.numpy as jnp
from jax import lax
from jax.experimental import pallas as pl
from jax.experimental.pallas import tpu as pltpu

EPS = 1e-5
TOP_K = 4
SWIGLU_LIMIT = 7.0
SWIGLU_ALPHA = 1.702
XATTN_HEAD_DIM_DIVISOR = 8

V7X_LANES = 128
V7X_SUBLANES = 8
V7X_VMEM_LIMIT_BYTES = 56 * 1024 * 1024

MIXER_ROWS = 512
INPROJ_ROWS = 1024
KV_ROWS = 512
ROUTE_TABLE_COLS = 2048
EXPERT_TILE_LOG2 = 8
EXPERT_UP_COLS = 1024
EXPERT_DOWN_COLS = 2048
PERMUTE_ROWS = 256

F32 = jnp.float32
BF16 = jnp.bfloat16


def _params(*sem):
    return pltpu.CompilerParams(dimension_semantics=sem, vmem_limit_bytes=V7X_VMEM_LIMIT_BYTES)


def _sigmoid(x):
    return 1.0 / (1.0 + jnp.exp(-x))


def _layer_norm(x, g, b):
    mu = jnp.mean(x, axis=-1, keepdims=True)
    xc = x - mu
    var = jnp.mean(xc * xc, axis=-1, keepdims=True)
    return xc * lax.rsqrt(var + EPS) * g + b


def _rms_norm(x, g):
    return x * lax.rsqrt(jnp.mean(x * x, axis=-1, keepdims=True) + EPS) * g


def _norm_matmul_kernel(x_ref, g_ref, w_ref, o_ref, xn_ref):
    @pl.when(pl.program_id(1) == 0)
    def _():
        xn_ref[...] = _rms_norm(x_ref[...], g_ref[...]).astype(BF16)

    o_ref[...] = jnp.dot(xn_ref[...], w_ref[...], preferred_element_type=F32).astype(o_ref.dtype)


def _norm_matmul(x2d, g, w_bf16, tm, tn):
    m, d = x2d.shape
    n = w_bf16.shape[1]
    return pl.pallas_call(
        _norm_matmul_kernel,
        grid=(m // tm, n // tn),
        in_specs=[
            pl.BlockSpec((tm, d), lambda i, j: (i, 0)),
            pl.BlockSpec((1, d), lambda i, j: (0, 0)),
            pl.BlockSpec((d, tn), lambda i, j: (0, j)),
        ],
        out_specs=pl.BlockSpec((tm, tn), lambda i, j: (i, j)),
        out_shape=jax.ShapeDtypeStruct((m, n), BF16),
        scratch_shapes=[pltpu.VMEM((tm, d), BF16)],
        compiler_params=_params("arbitrary", "arbitrary"),
        name="norm_matmul",
    )(x2d, g.reshape(1, d), w_bf16)


def _norm_matmul_split_kernel(x_ref, g_ref, w_ref, oa_ref, ob_ref, xn_ref):
    @pl.when(pl.program_id(1) == 0)
    def _():
        xn_ref[...] = _rms_norm(x_ref[...], g_ref[...]).astype(BF16)

    res = jnp.dot(xn_ref[...], w_ref[...], preferred_element_type=F32).astype(oa_ref.dtype)
    oa_ref[...] = res
    ob_ref[...] = res


def _norm_matmul_split(x2d, g, w_bf16, tm, tn, n_first):
    m, d = x2d.shape
    n_tiles = w_bf16.shape[1] // tn
    n_rest = n_tiles - n_first
    return pl.pallas_call(
        _norm_matmul_split_kernel,
        grid=(m // tm, n_tiles),
        in_specs=[
            pl.BlockSpec((tm, d), lambda i, j: (i, 0)),
            pl.BlockSpec((1, d), lambda i, j: (0, 0)),
            pl.BlockSpec((d, tn), lambda i, j: (0, j)),
        ],
        out_specs=[pl.BlockSpec((tm, tn), lambda i, j: (i, jnp.minimum(j, n_first))),
                   pl.BlockSpec((tm, tn), lambda i, j: (i, jnp.where(j >= n_first, j - n_first, n_rest)))],
        out_shape=[jax.ShapeDtypeStruct((m, (n_first + 1) * tn), BF16),
                   jax.ShapeDtypeStruct((m, (n_rest + 1) * tn), BF16)],
        scratch_shapes=[pltpu.VMEM((tm, d), BF16)],
        compiler_params=_params("arbitrary", "arbitrary"),
        name="norm_matmul_split",
    )(x2d, g.reshape(1, d), w_bf16)


CONV_HALO = 32
CONV_ROWS = 64
CONV_LANES = 256


def _conv_kernel(p_ref, q_ref, gl_ref, bg_ref, wdw_ref, bdw_ref, lg_ref, lb_ref, wo_ref, o_ref,
                 cp_ref, halo_ref, conv_ref, act_ref, *, tm, tiles_per_seq, kw):
    cc = p_ref.shape[1]
    back = (kw - 1) // V7X_SUBLANES * V7X_SUBLANES

    def branch():
        @pl.when(pl.program_id(0) % tiles_per_seq == 0)
        def _():
            halo_ref[...] = jnp.zeros_like(halo_ref)

        hg = p_ref[...].astype(F32) * _sigmoid(q_ref[...].astype(F32))
        halo = halo_ref[...]
        for r in range(V7X_SUBLANES):
            cp_ref[r, pl.ds(r, CONV_HALO), :] = halo
            cp_ref[r, pl.ds(CONV_HALO + r, tm), :] = hg
        halo_ref[...] = hg[tm - CONV_HALO:, :]

        for lc in range(cc // CONV_LANES):
            lanes = slice(lc * CONV_LANES, (lc + 1) * CONV_LANES)
            bias = bdw_ref[:, lanes]

            def body(rc, carry, lanes=lanes, bias=bias):
                base = rc * CONV_ROWS
                acc = jnp.zeros((CONV_ROWS // V7X_SUBLANES, V7X_SUBLANES, CONV_LANES), F32)
                for r in range(V7X_SUBLANES):
                    start = pl.multiple_of(base + (CONV_HALO - back), V7X_SUBLANES)
                    big = cp_ref[r, pl.ds(start, CONV_ROWS + back), lanes]
                    for k in range(kw):
                        delay = kw - 1 - k
                        if delay % V7X_SUBLANES != r:
                            continue
                        off = back - (delay - r)
                        win = big[off:off + CONV_ROWS].reshape(CONV_ROWS // V7X_SUBLANES, V7X_SUBLANES, CONV_LANES)
                        acc = acc + win * wdw_ref[k, :, lanes][None]
                conv_ref[pl.ds(pl.multiple_of(base, CONV_ROWS), CONV_ROWS), lanes] = (
                    acc.reshape(CONV_ROWS, CONV_LANES) + bias)
                return carry

            lax.fori_loop(0, tm // CONV_ROWS, body, 0)

        y = _layer_norm(conv_ref[...], lg_ref[...], lb_ref[...])
        act_ref[...] = (y * _sigmoid(y)).astype(BF16)

    branch()
    yo = jnp.dot(act_ref[...], wo_ref[...], preferred_element_type=F32)
    gate = _sigmoid(gl_ref[...].astype(F32) + bg_ref[...])
    o_ref[...] = (gate * yo).astype(o_ref.dtype)


def _gelu(x):
    return 0.5 * x * (1.0 + lax.erf(x * (1.0 / math.sqrt(2.0))))


def _sgu_term(u_ref, v_ref, gl_ref, bg_ref, lg_ref, lb_ref, ws_ref, bs_ref, wo_ref, s_ref, *, tm, groups, chunk):
    sw = u_ref.shape[1]
    gd = sw // groups
    nchunk = tm // chunk

    def branch():
        u = _gelu(u_ref[...].astype(F32))
        v = _gelu(v_ref[...].astype(F32))
        vb = _layer_norm(v, lg_ref[...], lb_ref[...]).astype(BF16)
        row = lax.broadcasted_iota(jnp.int32, (chunk, chunk), 0)
        col = lax.broadcasted_iota(jnp.int32, (chunk, chunk), 1)
        causal = row >= col
        for g in range(groups):
            lanes = slice(g * gd, (g + 1) * gd)
            wg = jnp.where(causal, ws_ref[g], 0.0).astype(BF16)
            vcat = jnp.concatenate([vb[c * chunk:(c + 1) * chunk, lanes] for c in range(nchunk)], axis=1)
            og = jnp.dot(wg, vcat, preferred_element_type=F32)
            for c in range(nchunk):
                rows = slice(c * chunk, (c + 1) * chunk)
                sv = og[:, c * gd:(c + 1) * gd] + bs_ref[:, lanes]
                s_ref[rows, lanes] = (u[rows, lanes] * sv).astype(BF16)

    branch()
    yo = jnp.dot(s_ref[...], wo_ref[...], preferred_element_type=F32)
    return _sigmoid(gl_ref[...].astype(F32) + bg_ref[...]) * yo


def _xattn_term(q_ref, k_ref, v_ref, gl_ref, bg_ref, wo_ref, att_ref, *, heads):
    xw = q_ref.shape[1]
    hd = xw // heads

    def branch():
        scale = hd ** -0.5
        for h in range(heads):
            lanes = slice(h * hd, (h + 1) * hd)
            s = lax.dot_general(q_ref[:, lanes], k_ref[:, lanes], (((1,), (1,)), ((), ())),
                                preferred_element_type=F32) * scale
            s = s - jnp.max(s, axis=-1, keepdims=True)
            e = jnp.exp(s)
            p = e / jnp.sum(e, axis=-1, keepdims=True)
            oh = jnp.dot(p.astype(BF16), v_ref[:, lanes], preferred_element_type=F32)
            att_ref[:, lanes] = oh.astype(BF16)

    branch()
    yo = jnp.dot(att_ref[...], wo_ref[...], preferred_element_type=F32)
    return _sigmoid(gl_ref[...].astype(F32) + bg_ref[...]) * yo


def _sgu_xattn_kernel(u_ref, v_ref, gl1_ref, bg1_ref, lg_ref, lb_ref, ws_ref, bs_ref, wo1_ref,
                      q_ref, k_ref, vm_ref, gl2_ref, bg2_ref, wo2_ref, acc_ref, o_ref, s_ref, att_ref,
                      *, tm, groups, chunk, heads):
    t_sgu = _sgu_term(u_ref, v_ref, gl1_ref, bg1_ref, lg_ref, lb_ref, ws_ref, bs_ref, wo1_ref, s_ref,
                      tm=tm, groups=groups, chunk=chunk)
    t_att = _xattn_term(q_ref, k_ref, vm_ref, gl2_ref, bg2_ref, wo2_ref, att_ref, heads=heads)
    o_ref[...] = (acc_ref[...].astype(F32) + t_sgu + t_att).astype(o_ref.dtype)


def _mix_route_kernel(m_ref, x_ref, wo_ref, g_ref, wrh_ref, wrl_ref, br_ref,
                      h_ref, n2_ref, idx_ref, prob_ref, rank_ref, cnt_ref, carry_ref, *, experts):
    tm = x_ref.shape[0]

    @pl.when(pl.program_id(0) == 0)
    def _():
        carry_ref[...] = jnp.zeros_like(carry_ref)

    h = x_ref[...] + jnp.dot(m_ref[...], wo_ref[...], preferred_element_type=F32)
    h_ref[...] = h
    n2 = _rms_norm(h, g_ref[...])
    n2_ref[...] = n2

    n_hi = n2.astype(BF16)
    n_lo = (n2 - n_hi.astype(F32)).astype(BF16)
    nt = (((1,), (1,)), ((), ()))
    lg = (lax.dot_general(wrh_ref[...], n_hi, nt, preferred_element_type=F32)
          + lax.dot_general(wrh_ref[...], n_lo, nt, preferred_element_type=F32)
          + lax.dot_general(wrl_ref[...], n_hi, nt, preferred_element_type=F32)
          + br_ref[...])

    eidx = lax.broadcasted_iota(jnp.int32, (experts, tm), 0)
    vals, onehots = [], []
    for k in range(TOP_K):
        mx = jnp.max(lg, axis=0, keepdims=True)
        ik = jnp.min(jnp.where(lg == mx, eidx, experts), axis=0, keepdims=True)
        sel = eidx == ik
        vals.append(mx)
        onehots.append(sel)
        idx_ref[k:k + 1, :] = ik
        lg = jnp.where(sel, -jnp.inf, lg)

    exps = [jnp.exp(v - vals[0]) for v in vals]
    denom = exps[0] + exps[1] + exps[2] + exps[3]
    for k in range(TOP_K):
        prob_ref[k:k + 1, :] = exps[k] / denom

    chosen = jnp.zeros((experts, tm), F32)
    for sel in onehots:
        chosen = chosen + sel.astype(F32)
    src = lax.broadcasted_iota(jnp.int32, (tm, tm), 0)
    dst = lax.broadcasted_iota(jnp.int32, (tm, tm), 1)
    before = (src < dst).astype(BF16)
    prior = jnp.dot(chosen.astype(BF16), before, preferred_element_type=F32) + carry_ref[...]
    for k in range(TOP_K):
        rk = jnp.sum(jnp.where(onehots[k], prior, 0.0), axis=0, keepdims=True)
        rank_ref[k:k + 1, :] = rk.astype(jnp.int32)
    carry_ref[...] = carry_ref[...] + jnp.sum(chosen, axis=1, keepdims=True)
    cnt_ref[...] = jnp.broadcast_to(carry_ref[...], cnt_ref.shape).astype(jnp.int32)


def _route_table_kernel(cnt_ref, idx_ref, rank_ref, dest_ref, meta_ref, *, experts, tile_log2):
    lanes = cnt_ref.shape[1]
    cnt = cnt_ref[...]
    padded = ((cnt + ((1 << tile_log2) - 1)) >> tile_log2) << tile_log2
    padded_f = padded.astype(F32)
    row = lax.broadcasted_iota(jnp.int32, (experts, lanes), 0)
    lane = lax.broadcasted_iota(jnp.int32, (experts, lanes), 1)
    padded_row = jnp.sum(jnp.where(row == lane, padded_f, 0.0), axis=0, keepdims=True)
    pstart_col = jnp.sum(jnp.where(lane < row, padded_row, 0.0), axis=1, keepdims=True)
    pend_col = pstart_col + padded_f[:, 0:1]

    tt = idx_ref.shape[1]
    eidx = lax.broadcasted_iota(jnp.int32, (experts, tt), 0)
    for k in range(TOP_K):
        sel = eidx == idx_ref[k:k + 1, :]
        start = jnp.sum(jnp.where(sel, pstart_col, 0.0), axis=0, keepdims=True)
        dest_ref[k:k + 1, :] = rank_ref[k:k + 1, :] + start.astype(jnp.int32)

    nb = meta_ref.shape[1]
    tile_start = (lax.broadcasted_iota(jnp.int32, (experts, nb), 1) << tile_log2).astype(F32)
    ecol = lax.broadcasted_iota(jnp.int32, (experts, nb), 0)
    be = jnp.minimum(jnp.sum((pend_col <= tile_start).astype(F32), axis=0, keepdims=True).astype(jnp.int32),
                     experts - 1)
    nonempty = padded_f[:, 0:1] > 0.0
    nxt = jnp.min(jnp.where(nonempty & (ecol > be), ecol, experts), axis=0, keepdims=True)
    first = jnp.min(jnp.where(nonempty, ecol, experts), axis=0, keepdims=True)
    total = jnp.sum(padded_f, axis=0, keepdims=True)[:, 0:1].astype(jnp.int32) >> tile_log2
    meta_ref[...] = jnp.zeros_like(meta_ref)
    meta_ref[0:1, :] = be
    meta_ref[1:2, :] = nxt
    meta_ref[2:3, :] = jnp.broadcast_to(total, (1, nb))
    meta_ref[3:4, :] = first
    pstart_row = jnp.sum(jnp.where(row < lane, padded_f, 0.0), axis=0, keepdims=True)
    cnt_row = jnp.sum(jnp.where(row == lane, cnt.astype(F32), 0.0), axis=0, keepdims=True)
    meta_ref[4:5, 0:lanes] = pstart_row.astype(jnp.int32)
    meta_ref[5:6, 0:lanes] = cnt_row.astype(jnp.int32)


def _dispatch_kernel(ps_ref, cnt_ref, nu_ref, dest_ref, n2_ref, xp_ref, zero_ref, sem, zsem,
                     *, experts, n_blocks, tile_log2, n_assign):
    tm = n2_ref.shape[0]
    te = 1 << tile_log2

    def pad_copy(row):
        return pltpu.make_async_copy(zero_ref.at[pl.ds(0, 1)], xp_ref.at[pl.ds(row, 1)], zsem)

    def tile_copy(b):
        return pltpu.make_async_copy(zero_ref, xp_ref.at[pl.ds(pl.multiple_of(b * te, te), te)], zsem)

    @pl.when(pl.program_id(0) == 0)
    def _():
        zero_ref[...] = jnp.zeros_like(zero_ref)

        def per_expert(e, carry):
            first = ps_ref[e] + cnt_ref[e]
            n = ((cnt_ref[e] + (te - 1)) >> tile_log2 << tile_log2) - cnt_ref[e]
            lax.fori_loop(0, n, lambda r, c: (pad_copy(first + r).start(), c)[1], 0)
            return carry

        lax.fori_loop(0, experts, per_expert, 0)
        lax.fori_loop(nu_ref[0], n_blocks, lambda b, c: (tile_copy(b).start(), c)[1], 0)

    def row_copy(t, d):
        return pltpu.make_async_copy(n2_ref.at[pl.ds(t, 1)], xp_ref.at[pl.ds(d, 1)], sem)

    def issue(t, carry):
        for k in range(TOP_K):
            row_copy(t, dest_ref[k, t]).start()
        return carry

    lax.fori_loop(0, tm, issue, 0)

    for k in range(TOP_K):
        pltpu.make_async_copy(n2_ref, xp_ref.at[pl.ds(0, tm)], sem).wait()

    @pl.when(pl.program_id(0) == pl.num_programs(0) - 1)
    def _():
        pad_total = (nu_ref[0] << tile_log2) - n_assign
        n_tile_waits = (pad_total >> tile_log2) + (n_blocks - nu_ref[0])
        lax.fori_loop(0, n_tile_waits, lambda b, c: (tile_copy(0).wait(), c)[1], 0)
        lax.fori_loop(0, pad_total & (te - 1), lambda r, c: (pad_copy(0).wait(), c)[1], 0)


def _stream_expert_weights(be_ref, nx_ref, fe_ref, copies, load, *, experts, n_pass):
    p = pl.program_id(0)
    m = pl.program_id(1)
    e = be_ref[m]

    @pl.when((m == 0) | (e != be_ref[jnp.maximum(m - 1, 0)]))
    def _():
        @pl.when((m == 0) & (p == 0))
        def _():
            for c in copies(e, p):
                c.start()

        for c in copies(e, p):
            c.wait()
        load()
        same_pass = nx_ref[m] < experts
        e2 = jnp.where(same_pass, nx_ref[m], fe_ref[0])
        p2 = jnp.where(same_pass, p, p + 1)

        @pl.when(p2 < n_pass)
        def _():
            for c in copies(e2, p2):
                c.start(priority=1)


WEIGHT_CAST_ROWS = 128


def _cast_rows(dst_ref, src_ref):
    def body(i, carry):
        rows = pl.ds(pl.multiple_of(i * WEIGHT_CAST_ROWS, WEIGHT_CAST_ROWS), WEIGHT_CAST_ROWS)
        dst_ref[rows, :] = src_ref[rows, :].astype(BF16)
        return carry

    lax.fori_loop(0, dst_ref.shape[0] // WEIGHT_CAST_ROWS, body, 0)


def _expert_up_kernel(be_ref, nx_ref, nu_ref, fe_ref, x_ref, bg_ref, bu_ref, w_hbm, o_ref,
                      stage_ref, wgb_ref, wub_ref, sem, *, experts, n_pass):
    m = pl.program_id(1)
    tn = o_ref.shape[1]
    de = n_pass * tn

    def copies(e, p):
        return [pltpu.make_async_copy(w_hbm.at[e, :, pl.ds(pl.multiple_of(half * de + p * tn, tn), tn)],
                                      stage_ref.at[half], sem.at[half]) for half in range(2)]

    def load():
        _cast_rows(wgb_ref, stage_ref.at[0])
        _cast_rows(wub_ref, stage_ref.at[1])

    @pl.when(m < nu_ref[0])
    def _():
        _stream_expert_weights(be_ref, nx_ref, fe_ref, copies, load, experts=experts, n_pass=n_pass)
        x = x_ref[...].astype(BF16)
        g = jnp.dot(x, wgb_ref[...], preferred_element_type=F32) + bg_ref[0]
        u = jnp.dot(x, wub_ref[...], preferred_element_type=F32) + bu_ref[0]
        g = jnp.minimum(g, SWIGLU_LIMIT)
        u = jnp.clip(u, -SWIGLU_LIMIT, SWIGLU_LIMIT)
        o_ref[...] = ((u + 1.0) * (g * _sigmoid(SWIGLU_ALPHA * g))).astype(o_ref.dtype)

    @pl.when(m >= nu_ref[0])
    def _():
        o_ref[...] = jnp.zeros_like(o_ref)


def _expert_down_kernel(be_ref, nx_ref, nu_ref, fe_ref, h_ref, b_ref, w_hbm, o_ref, stage_ref, wb_ref, sem,
                        *, experts, n_pass):
    m = pl.program_id(1)
    tn = o_ref.shape[1]

    def copies(e, p):
        return [pltpu.make_async_copy(w_hbm.at[e, :, pl.ds(pl.multiple_of(p * tn, tn), tn)], stage_ref, sem)]

    def load():
        _cast_rows(wb_ref, stage_ref)

    @pl.when(m < nu_ref[0])
    def _():
        _stream_expert_weights(be_ref, nx_ref, fe_ref, copies, load, experts=experts, n_pass=n_pass)
        o_ref[...] = jnp.dot(h_ref[...], wb_ref[...], preferred_element_type=F32) + b_ref[0]

    @pl.when(m >= nu_ref[0])
    def _():
        o_ref[...] = jnp.zeros_like(o_ref)


def _combine_kernel(dest_ref, dnext_ref, p_ref, h_ref, g_ref, yp_ref, o_ref, rows_ref, sem, *, n_tiles):
    i = pl.program_id(0)
    tm = h_ref.shape[0]
    slot = i % 2

    def row_copy(idx_ref, buf, t, k):
        return pltpu.make_async_copy(yp_ref.at[pl.ds(idx_ref[k, t], 1)], rows_ref.at[buf, k, pl.ds(t, 1)],
                                     sem.at[buf])

    def for_each_row(idx_ref, buf, fn):
        def body(t, carry):
            for k in range(TOP_K):
                fn(row_copy(idx_ref, buf, t, k), k)
            return carry

        lax.fori_loop(0, tm, body, 0)

    start = lambda c, k: c.start()

    @pl.when(i == 0)
    def _():
        for_each_row(dest_ref, 0, start)

    @pl.when(i + 1 < n_tiles)
    def _():
        for_each_row(dnext_ref, 1 - slot, start)

    for k in range(TOP_K):
        pltpu.make_async_copy(yp_ref.at[pl.ds(0, tm)], rows_ref.at[slot, k], sem.at[slot]).wait()

    y = h_ref[...]
    for k in range(TOP_K):
        y = y + p_ref[:, k:k + 1] * rows_ref[slot, k]
    o_ref[...] = _rms_norm(y, g_ref[...])


def kernel(x, mem, norm1_g, w_in, b_gate, w_dw, b_dw, conv_ln_g, conv_ln_b, w_conv_out, sgu_ln_g, sgu_ln_b,
           w_spatial, b_spatial, w_sgu_out, mem_norm_g, w_kv, w_xattn_out, w_out, norm2_g, w_router, b_router,
           w_gate_up, b_gate_up, w_down, b_down, final_norm_g):
    b, s, d = x.shape
    depth = norm1_g.shape[0]
    t = b * s
    nm = mem.shape[1]
    kw, cc = w_dw.shape[1], w_dw.shape[2]
    sw = w_sgu_out.shape[1]
    groups, chunk = w_spatial.shape[1], w_spatial.shape[2]
    xw = w_xattn_out.shape[1]
    heads = xw // (d // XATTN_HEAD_DIM_DIVISOR)
    experts = w_router.shape[2]
    de = w_down.shape[2]
    in_cols = w_in.shape[2]

    unit = math.gcd(math.gcd(cc, sw), math.gcd(xw, d))
    assert in_cols == 2 * cc + 2 * sw + xw + 3 * d and in_cols % unit == 0
    assert (sw // groups) % V7X_LANES == 0 and kw - 1 <= CONV_HALO and cc % CONV_LANES == 0
    assert (2 * cc) % sw == 0 and (2 * cc + 2 * sw) % xw == 0 and d % 2 == 0

    tm = min(MIXER_ROWS, s)
    assert s % tm == 0 and tm % chunk == 0 and tm % CONV_ROWS == 0
    tiles_per_seq = s // tm
    gate_col0 = (2 * cc + 2 * sw + xw) // unit

    h = x.reshape(t, d)
    for l in range(depth):
        proj, gates = _norm_matmul_split(h, norm1_g[l], w_in[l].astype(BF16), min(INPROJ_ROWS, t), unit, gate_col0)
        kv = _norm_matmul(mem.reshape(b * nm, d), mem_norm_g[l], w_kv[l].astype(BF16), min(KV_ROWS, b * nm), xw)
        bg = b_gate[l].reshape(1, 3 * d)

        def gate_specs(branch):
            return [pl.BlockSpec((tm, d), lambda i: (i, branch)), pl.BlockSpec((1, d), lambda i: (0, branch))]

        tile_spec = pl.BlockSpec((tm, d), lambda i: (i, 0))
        row_vec = lambda n: pl.BlockSpec((1, n), lambda i: (0, 0))
        whole = lambda shape: pl.BlockSpec(shape, lambda i: tuple(0 for _ in shape))

        wdw_b = jnp.broadcast_to(w_dw[l][:, None, :], (kw, V7X_SUBLANES, cc))
        gy = pl.pallas_call(
            functools.partial(_conv_kernel, tm=tm, tiles_per_seq=tiles_per_seq, kw=kw),
            grid=(t // tm,),
            in_specs=[pl.BlockSpec((tm, cc), lambda i: (i, 0)),
                      pl.BlockSpec((tm, cc), lambda i: (i, 1)),
                      *gate_specs(0),
                      whole((kw, V7X_SUBLANES, cc)),
                      row_vec(cc), row_vec(cc), row_vec(cc),
                      whole((cc, d))],
            out_specs=tile_spec,
            out_shape=jax.ShapeDtypeStruct((t, d), BF16),
            scratch_shapes=[pltpu.VMEM((V7X_SUBLANES, tm + CONV_HALO + V7X_SUBLANES, cc), F32),
                            pltpu.VMEM((CONV_HALO, cc), F32),
                            pltpu.VMEM((tm, cc), F32),
                            pltpu.VMEM((tm, cc), BF16)],
            compiler_params=_params("arbitrary"),
            name="conv_branch",
        )(proj, proj, gates, bg, wdw_b, b_dw[l].reshape(1, cc), conv_ln_g[l].reshape(1, cc),
          conv_ln_b[l].reshape(1, cc), w_conv_out[l].astype(BF16))

        bs_full = jnp.repeat(b_spatial[l].T, sw // groups, axis=1)
        merged = pl.pallas_call(
            functools.partial(_sgu_xattn_kernel, tm=tm, groups=groups, chunk=chunk, heads=heads),
            grid=(t // tm,),
            in_specs=[pl.BlockSpec((tm, sw), lambda i: (i, 2 * cc // sw)),
                      pl.BlockSpec((tm, sw), lambda i: (i, 2 * cc // sw + 1)),
                      *gate_specs(1),
                      row_vec(sw), row_vec(sw),
                      whole((groups, chunk, chunk)),
                      whole((chunk, sw)),
                      whole((sw, d)),
                      pl.BlockSpec((tm, xw), lambda i: (i, (2 * cc + 2 * sw) // xw)),
                      pl.BlockSpec((nm, xw), lambda i: (i // tiles_per_seq, 0)),
                      pl.BlockSpec((nm, xw), lambda i: (i // tiles_per_seq, 1)),
                      *gate_specs(2),
                      whole((xw, d)),
                      tile_spec],
            out_specs=tile_spec,
            out_shape=jax.ShapeDtypeStruct((t, d), BF16),
            scratch_shapes=[pltpu.VMEM((tm, sw), BF16), pltpu.VMEM((tm, xw), BF16)],
            compiler_params=_params("arbitrary"),
            name="sgu_xattn_branch",
        )(proj, proj, gates, bg, sgu_ln_g[l].reshape(1, sw), sgu_ln_b[l].reshape(1, sw),
          w_spatial[l], bs_full, w_sgu_out[l].astype(BF16),
          proj, kv, kv, gates, bg, w_xattn_out[l].astype(BF16), gy)

        wr = w_router[l].T
        wr_hi = wr.astype(BF16)
        wr_lo = (wr - wr_hi.astype(F32)).astype(BF16)
        tr = min(MIXER_ROWS, t)
        full = lambda shape: pl.BlockSpec(shape, lambda i: tuple(0 for _ in shape))
        kt_spec = pl.BlockSpec((TOP_K, tr), lambda i: (0, i))
        h, n2p, top_idx, probs, rank, counts = pl.pallas_call(
            functools.partial(_mix_route_kernel, experts=experts),
            grid=(t // tr,),
            in_specs=[pl.BlockSpec((tr, d), lambda i: (i, 0)),
                      pl.BlockSpec((tr, d), lambda i: (i, 0)),
                      full((d, d)), full((1, d)), full((experts, d)), full((experts, d)), full((experts, 1))],
            out_specs=[pl.BlockSpec((tr, d), lambda i: (i, 0)),
                       pl.BlockSpec((tr, d), lambda i: (i, 0)),
                       kt_spec, kt_spec, kt_spec,
                       full((experts, V7X_LANES))],
            out_shape=[jax.ShapeDtypeStruct((t, d), F32),
                       jax.ShapeDtypeStruct((t, d), F32),
                       jax.ShapeDtypeStruct((TOP_K, t), jnp.int32),
                       jax.ShapeDtypeStruct((TOP_K, t), F32),
                       jax.ShapeDtypeStruct((TOP_K, t), jnp.int32),
                       jax.ShapeDtypeStruct((experts, V7X_LANES), jnp.int32)],
            scratch_shapes=[pltpu.VMEM((experts, 1), F32)],
            compiler_params=_params("arbitrary"),
            name="mix_route",
        )(merged, h, w_out[l].astype(BF16), norm2_g[l].reshape(1, d), wr_hi, wr_lo, b_router[l].reshape(experts, 1))

        tile_log2 = EXPERT_TILE_LOG2
        te = 1 << tile_log2
        n_blocks = (t * TOP_K) // te + experts
        cap = n_blocks * te
        nb_pad = -(-n_blocks // V7X_LANES) * V7X_LANES
        tt = min(ROUTE_TABLE_COLS, t)
        dest, meta = pl.pallas_call(
            functools.partial(_route_table_kernel, experts=experts, tile_log2=tile_log2),
            grid=(t // tt,),
            in_specs=[pl.BlockSpec((experts, V7X_LANES), lambda i: (0, 0)),
                      pl.BlockSpec((TOP_K, tt), lambda i: (0, i)),
                      pl.BlockSpec((TOP_K, tt), lambda i: (0, i))],
            out_specs=[pl.BlockSpec((TOP_K, tt), lambda i: (0, i)),
                       pl.BlockSpec((V7X_SUBLANES, nb_pad), lambda i: (0, 0))],
            out_shape=[jax.ShapeDtypeStruct((TOP_K, t), jnp.int32),
                       jax.ShapeDtypeStruct((V7X_SUBLANES, nb_pad), jnp.int32)],
            compiler_params=_params("arbitrary"),
            name="route_table",
        )(counts, top_idx, rank)
        block_e, next_e = meta[0, :n_blocks], meta[1, :n_blocks]
        n_used, first_e = meta[2, :1], meta[3, :1]

        td = min(PERMUTE_ROWS, t)
        xp = pl.pallas_call(
            functools.partial(_dispatch_kernel, experts=experts, n_blocks=n_blocks, tile_log2=tile_log2,
                              n_assign=t * TOP_K),
            grid_spec=pltpu.PrefetchScalarGridSpec(
                num_scalar_prefetch=3,
                grid=(t // td,),
                in_specs=[pl.BlockSpec((TOP_K, td), lambda i, ps, cn, nu: (0, i), memory_space=pltpu.SMEM),
                          pl.BlockSpec((td, d), lambda i, ps, cn, nu: (i, 0))],
                out_specs=pl.BlockSpec(memory_space=pl.ANY),
                scratch_shapes=[pltpu.VMEM((te, d), F32),
                                pltpu.SemaphoreType.DMA(()), pltpu.SemaphoreType.DMA(())]),
            out_shape=jax.ShapeDtypeStruct((cap, d), F32),
            compiler_params=_params("arbitrary"),
            name="dispatch",
        )(meta[4, :experts], meta[5, :experts], n_used, dest, n2p)

        tn_up = min(EXPERT_UP_COLS, de)
        n_up = de // tn_up
        clamp = lambda m, nu: jnp.minimum(m, nu[0] - 1)
        bias_up = b_gate_up[l].reshape(experts, 1, 2 * de)
        hidden = pl.pallas_call(
            functools.partial(_expert_up_kernel, experts=experts, n_pass=n_up),
            grid_spec=pltpu.PrefetchScalarGridSpec(
                num_scalar_prefetch=4,
                grid=(n_up, n_blocks),
                in_specs=[pl.BlockSpec((te, d), lambda p, m, be, nx, nu, fe: (clamp(m, nu), 0)),
                          pl.BlockSpec((1, 1, tn_up), lambda p, m, be, nx, nu, fe: (be[clamp(m, nu)], 0, p)),
                          pl.BlockSpec((1, 1, tn_up),
                                       lambda p, m, be, nx, nu, fe: (be[clamp(m, nu)], 0, n_up + p)),
                          pl.BlockSpec(memory_space=pl.ANY)],
                out_specs=pl.BlockSpec((te, tn_up), lambda p, m, be, nx, nu, fe: (m, p)),
                scratch_shapes=[pltpu.VMEM((2, d, tn_up), F32),
                                pltpu.VMEM((d, tn_up), BF16), pltpu.VMEM((d, tn_up), BF16),
                                pltpu.SemaphoreType.DMA((2,))]),
            out_shape=jax.ShapeDtypeStruct((cap, de), BF16),
            compiler_params=_params("arbitrary", "arbitrary"),
            name="expert_up",
        )(block_e, next_e, n_used, first_e, xp, bias_up, bias_up, w_gate_up[l])

        tn_dn = min(EXPERT_DOWN_COLS, d)
        n_dn = d // tn_dn
        yp = pl.pallas_call(
            functools.partial(_expert_down_kernel, experts=experts, n_pass=n_dn),
            grid_spec=pltpu.PrefetchScalarGridSpec(
                num_scalar_prefetch=4,
                grid=(n_dn, n_blocks),
                in_specs=[pl.BlockSpec((te, de), lambda p, m, be, nx, nu, fe: (clamp(m, nu), 0)),
                          pl.BlockSpec((1, 1, tn_dn), lambda p, m, be, nx, nu, fe: (be[clamp(m, nu)], 0, p)),
                          pl.BlockSpec(memory_space=pl.ANY)],
                out_specs=pl.BlockSpec((te, tn_dn), lambda p, m, be, nx, nu, fe: (m, p)),
                scratch_shapes=[pltpu.VMEM((de, tn_dn), F32), pltpu.VMEM((de, tn_dn), BF16),
                                pltpu.SemaphoreType.DMA(())]),
            out_shape=jax.ShapeDtypeStruct((cap, d), F32),
            compiler_params=_params("arbitrary", "arbitrary"),
            name="expert_down",
        )(block_e, next_e, n_used, first_e, hidden, b_down[l].reshape(experts, 1, d), w_down[l])

        assert depth == 1
        tc = min(PERMUTE_ROWS, t)
        n_tc = t // tc
        h = pl.pallas_call(
            functools.partial(_combine_kernel, n_tiles=n_tc),
            grid=(n_tc,),
            in_specs=[pl.BlockSpec((TOP_K, tc), lambda i: (0, i), memory_space=pltpu.SMEM),
                      pl.BlockSpec((TOP_K, tc), lambda i: (0, jnp.minimum(i + 1, n_tc - 1)),
                                   memory_space=pltpu.SMEM),
                      pl.BlockSpec((tc, TOP_K), lambda i: (i, 0)),
                      pl.BlockSpec((tc, d), lambda i: (i, 0)),
                      pl.BlockSpec((1, d), lambda i: (0, 0)),
                      pl.BlockSpec(memory_space=pl.ANY)],
            out_specs=pl.BlockSpec((tc, d), lambda i: (i, 0)),
            out_shape=jax.ShapeDtypeStruct((t, d), F32),
            scratch_shapes=[pltpu.VMEM((2, TOP_K, tc, d), F32), pltpu.SemaphoreType.DMA((2,))],
            compiler_params=_params("arbitrary"),
            name="combine",
        )(dest, dest, probs.T, h, final_norm_g.reshape(1, d), yp)

    return h.reshape(b, s, d)
```
